```python
import jax, jax.numpy as jnp
from jax import lax
import numpy as np

D_MODEL = 1024
BATCH = 4
SEQ = 8192
DEPTH = 1

PLE_DIM = 256
MOBA_HEADS = 8
MOBA_HEAD_DIM = 64
MOBA_BLOCK = 256
MOBA_TOPK = 3
MOBA_Q_CHUNK = 32
ROPE_THETA = 500000.0
PARTIAL_ROT = MOBA_HEAD_DIM // 4
MOBA_W = MOBA_HEADS * MOBA_HEAD_DIM
MLA_HEADS = 8
MLA_Q_LORA = 512
MLA_KV_LORA = 256
MLA_NOPE = 64
MLA_ROPE = 32
MLA_V = 64
MLA_QK = MLA_NOPE + MLA_ROPE
MLA_ROPE_THETA = 10000.0
ATTN_Q_BLOCK = 128
PEER_HEADS = 8
PEER_N_KEYS = 128
PEER_N_EXPERTS = PEER_N_KEYS * PEER_N_KEYS
PEER_KEY_DIM = 256
PEER_TOPK = 16
PEER_TOKEN_CHUNK = 128
N_BRANCH = 2
IN_SPLITS = [MOBA_W, 2 * MOBA_W, 3 * MOBA_W, 3 * MOBA_W + MLA_Q_LORA,
             3 * MOBA_W + MLA_Q_LORA + MLA_KV_LORA,
             3 * MOBA_W + MLA_Q_LORA + MLA_KV_LORA + MLA_ROPE,
             3 * MOBA_W + MLA_Q_LORA + MLA_KV_LORA + MLA_ROPE + D_MODEL]
IN_COLS = 3 * MOBA_W + MLA_Q_LORA + MLA_KV_LORA + MLA_ROPE + N_BRANCH * D_MODEL
DEEPNORM_ALPHA = (2.0 * DEPTH) ** 0.25
DEEPNORM_BETA = (8.0 * DEPTH) ** -0.25
NEG_INF = -1e30
LN_EPS = 1e-5
RMS_EPS = 1e-6

kernel_name = "hybrid_moba_mla_peer_deepnorm_block"


def layer_norm(x, g, b):
    xf = x.astype(jnp.float32)
    mu = jnp.mean(xf, axis=-1, keepdims=True)
    var = jnp.mean(jnp.square(xf - mu), axis=-1, keepdims=True)
    y = (xf - mu) * lax.rsqrt(var + LN_EPS) * g.astype(jnp.float32) + b.astype(jnp.float32)
    return y.astype(x.dtype)


def rms_norm(x, g):
    xf = x.astype(jnp.float32)
    y = xf * lax.rsqrt(jnp.mean(xf * xf, axis=-1, keepdims=True) + RMS_EPS) * g.astype(jnp.float32)
    return y.astype(x.dtype)


def rope_tables(pos, rot_dim, theta):
    inv = jnp.power(jnp.float32(theta), -jnp.arange(0, rot_dim, 2, dtype=jnp.float32) / rot_dim)
    ang = pos.astype(jnp.float32)[:, None] * inv[None, :]
    ang = jnp.concatenate([ang, ang], axis=-1)
    return jnp.cos(ang), jnp.sin(ang)


def apply_rope(x, cos, sin):
    r = cos.shape[-1]
    xr, xp = x[..., :r], x[..., r:]
    x1, x2 = xr[..., : r // 2], xr[..., r // 2:]
    rot = jnp.concatenate([-x2, x1], axis=-1)
    c = cos[:, None, :].astype(x.dtype)
    s = sin[:, None, :].astype(x.dtype)
    return jnp.concatenate([xr * c + rot * s, xp], axis=-1)


def moba_attention(q, k, v):
    B, H, S, dh = q.shape
    nb = -(-S // MOBA_BLOCK)
    pad = nb * MOBA_BLOCK - S
    kp = jnp.pad(k, ((0, 0), (0, 0), (0, pad), (0, 0))).reshape(B, H, nb, MOBA_BLOCK, dh)
    vp = jnp.pad(v, ((0, 0), (0, 0), (0, pad), (0, 0))).reshape(B, H, nb, MOBA_BLOCK, dh)
    k_mean = jnp.mean(kp.astype(jnp.float32), axis=3)
    n_sel = min(MOBA_TOPK, nb)
    scale = dh ** -0.5
    bi = jnp.arange(B)[:, None, None, None]
    hi = jnp.arange(H)[None, :, None, None]
    blk_ids = jnp.arange(nb)

    def chunk(c):
        start = c * MOBA_Q_CHUNK
        qc = lax.dynamic_slice_in_dim(q, start, MOBA_Q_CHUNK, axis=2)
        qpos = start + jnp.arange(MOBA_Q_CHUNK)
        qblk = start // MOBA_BLOCK
        gate = jnp.einsum('bhqd,bhnd->bhqn', qc.astype(jnp.float32), k_mean)
        gate = jnp.where(blk_ids < qblk, gate, NEG_INF)
        _, sel = lax.top_k(gate, n_sel)
        sel_valid = jnp.repeat(sel < qblk, MOBA_BLOCK, axis=-1)
        k_sel = kp[bi, hi, sel].reshape(B, H, MOBA_Q_CHUNK, n_sel * MOBA_BLOCK, dh)
        v_sel = vp[bi, hi, sel].reshape(B, H, MOBA_Q_CHUNK, n_sel * MOBA_BLOCK, dh)
        s_sel = jnp.einsum('bhqd,bhqkd->bhqk', qc, k_sel).astype(jnp.float32) * scale
        s_sel = jnp.where(sel_valid, s_sel, NEG_INF)
        k_own = lax.dynamic_index_in_dim(kp, qblk, axis=2, keepdims=False)
        v_own = lax.dynamic_index_in_dim(vp, qblk, axis=2, keepdims=False)
        kpos = qblk * MOBA_BLOCK + jnp.arange(MOBA_BLOCK)
        s_own = jnp.einsum('bhqd,bhkd->bhqk', qc, k_own).astype(jnp.float32) * scale
        s_own = jnp.where(kpos[None, :] <= qpos[:, None], s_own, NEG_INF)
        probs = jax.nn.softmax(jnp.concatenate([s_sel, s_own], axis=-1), axis=-1).astype(v.dtype)
        p_sel = probs[..., : n_sel * MOBA_BLOCK]
        p_own = probs[..., n_sel * MOBA_BLOCK:]
        return (jnp.einsum('bhqk,bhqkd->bhqd', p_sel, v_sel)
                + jnp.einsum('bhqk,bhkd->bhqd', p_own, v_own))

    out = lax.map(chunk, jnp.arange(S // MOBA_Q_CHUNK))
    return out.transpose(1, 0, 3, 2, 4).reshape(B, S, H * dh)


def causal_attention(q, k, v):
    B, H, S, dk = q.shape
    dv = v.shape[-1]
    nq = S // ATTN_Q_BLOCK
    qb = q.reshape(B, H, nq, ATTN_Q_BLOCK, dk).transpose(2, 0, 1, 3, 4)
    kpos = jnp.arange(S)
    scale = dk ** -0.5

    def block(args):
        qi, i = args
        qpos = i * ATTN_Q_BLOCK + jnp.arange(ATTN_Q_BLOCK)
        s = jnp.einsum('bhqd,bhkd->bhqk', qi, k).astype(jnp.float32) * scale
        s = jnp.where(kpos[None, :] <= qpos[:, None], s, NEG_INF)
        pr = jax.nn.softmax(s, axis=-1).astype(v.dtype)
        return jnp.einsum('bhqk,bhkd->bhqd', pr, v)

    out = lax.map(block, (qb, jnp.arange(nq)))
    return out.transpose(1, 0, 3, 2, 4).reshape(B, S, H * dv)


def peer_ffn(h, w_q, sub_keys, u_tab, v_tab):
    B, S, D = h.shape
    T = B * S
    ht = h.reshape(T, D)
    q = (ht @ w_q).reshape(T, PEER_HEADS, 2, PEER_KEY_DIM // 2)
    s_half = jnp.einsum('thpd,hpnd->thpn', q, sub_keys).astype(jnp.float32)
    top_s, top_i = lax.top_k(s_half, PEER_TOPK)
    cand_s = (top_s[:, :, 0, :, None] + top_s[:, :, 1, None, :]).reshape(T, PEER_HEADS, PEER_TOPK * PEER_TOPK)
    cand_i = (top_i[:, :, 0, :, None] * PEER_N_KEYS + top_i[:, :, 1, None, :]).reshape(T, PEER_HEADS, PEER_TOPK * PEER_TOPK)
    best_s, best_pos = lax.top_k(cand_s, PEER_TOPK)
    experts = jnp.take_along_axis(cand_i, best_pos, axis=-1)
    gates = jax.nn.softmax(best_s, axis=-1).astype(h.dtype)
    nc = T // PEER_TOKEN_CHUNK
    xs = (ht.reshape(nc, PEER_TOKEN_CHUNK, D),
          experts.reshape(nc, PEER_TOKEN_CHUNK, PEER_HEADS * PEER_TOPK),
          gates.reshape(nc, PEER_TOKEN_CHUNK, PEER_HEADS * PEER_TOPK))

    def chunk(args):
        xc, ec, gc = args
        a = jnp.einsum('td,ted->te', xc, u_tab[ec])
        w = gc * jax.nn.gelu(a, approximate=False)
        return jnp.einsum('te,ted->td', w, v_tab[ec])

    out = lax.map(chunk, xs)
    return out.reshape(B, S, D)


def setup_inputs(seed: int = 0) -> dict:
    key = jax.random.key(seed)
    ks = jax.random.split(key, 24)
    f32 = jnp.float32
    L, D = DEPTH, D_MODEL

    def nrm(k, shape, scale):
        return jax.random.normal(k, shape, f32) * scale

    return {
        "x": nrm(ks[0], (BATCH, SEQ, D), 1.0),
        "p": nrm(ks[1], (DEPTH, BATCH, SEQ, PLE_DIM), 1.0),
        "w_in": nrm(ks[2], (L, D, IN_COLS), D ** -0.5),
        "b_gates": nrm(ks[3], (L, N_BRANCH, D), 0.02),
        "mla_q_norm": 1.0 + nrm(ks[4], (L, MLA_Q_LORA), 0.02),
        "w_q_up": nrm(ks[5], (L, MLA_Q_LORA, MLA_HEADS * MLA_QK), MLA_Q_LORA ** -0.5),
        "mla_kv_norm": 1.0 + nrm(ks[6], (L, MLA_KV_LORA), 0.02),
        "w_kv_up": nrm(ks[7], (L, MLA_KV_LORA, MLA_HEADS * (MLA_NOPE + MLA_V)), MLA_KV_LORA ** -0.5),
        "w_branch_moba": nrm(ks[8], (L, MOBA_W, D), MOBA_W ** -0.5),
        "w_branch_mla": nrm(ks[9], (L, MLA_HEADS * MLA_V, D), (MLA_HEADS * MLA_V) ** -0.5),
        "w_out": nrm(ks[10], (L, D, D), D ** -0.5 * DEEPNORM_BETA),
        "ln1_g": 1.0 + nrm(ks[11], (L, D), 0.02),
        "ln1_b": nrm(ks[12], (L, D), 0.02),
        "peer_w_q": nrm(ks[13], (L, D, PEER_HEADS * PEER_KEY_DIM), D ** -0.5),
        "peer_sub_keys": nrm(ks[14], (L, PEER_HEADS, 2, PEER_N_KEYS, PEER_KEY_DIM // 2), (PEER_KEY_DIM // 2) ** -0.5),
        "peer_u": nrm(ks[15], (L, PEER_N_EXPERTS, D), D ** -0.5),
        "peer_v": nrm(ks[16], (L, PEER_N_EXPERTS, D), DEEPNORM_BETA),
        "w_ple_proj": nrm(ks[17], (L, PLE_DIM, D), PLE_DIM ** -0.5 * DEEPNORM_BETA),
        "w_ple_gate": nrm(ks[18], (L, D, D), D ** -0.5),
        "b_ple_gate": nrm(ks[19], (L, D), 0.02),
        "ln2_g": 1.0 + nrm(ks[20], (L, D), 0.02),
        "ln2_b": nrm(ks[21], (L, D), 0.02),
    }


def reference(x, p, w_in, b_gates, mla_q_norm, w_q_up, mla_kv_norm, w_kv_up, w_branch_moba,
              w_branch_mla, w_out, ln1_g, ln1_b, peer_w_q, peer_sub_keys, peer_u, peer_v,
              w_ple_proj, w_ple_gate, b_ple_gate, ln2_g, ln2_b):
    B, S, _ = x.shape
    pos = jnp.arange(S)
    cos_p, sin_p = rope_tables(pos, PARTIAL_ROT, ROPE_THETA)
    cos_m, sin_m = rope_tables(pos, MLA_ROPE, MLA_ROPE_THETA)
    for i in range(DEPTH):
        proj = x @ w_in[i]
        q_a, k_a, v_a, c_q, c_kv, k_r, g_a, g_b = jnp.split(proj, IN_SPLITS, axis=-1)

        q_a = apply_rope(q_a.reshape(B, S, MOBA_HEADS, MOBA_HEAD_DIM), cos_p, sin_p)
        k_a = apply_rope(k_a.reshape(B, S, MOBA_HEADS, MOBA_HEAD_DIM), cos_p, sin_p)
        v_a = v_a.reshape(B, S, MOBA_HEADS, MOBA_HEAD_DIM)
        y_a = moba_attention(q_a.transpose(0, 2, 1, 3), k_a.transpose(0, 2, 1, 3), v_a.transpose(0, 2, 1, 3))

        c_q = rms_norm(c_q, mla_q_norm[i])
        q_m = (c_q @ w_q_up[i]).reshape(B, S, MLA_HEADS, MLA_QK)
        q_m = jnp.concatenate([q_m[..., :MLA_NOPE], apply_rope(q_m[..., MLA_NOPE:], cos_m, sin_m)], axis=-1)
        c_kv = rms_norm(c_kv, mla_kv_norm[i])
        kv = (c_kv @ w_kv_up[i]).reshape(B, S, MLA_HEADS, MLA_NOPE + MLA_V)
        k_nope, v_m = kv[..., :MLA_NOPE], kv[..., MLA_NOPE:]
        k_rope = apply_rope(k_r.reshape(B, S, 1, MLA_ROPE), cos_m, sin_m)
        k_m = jnp.concatenate([k_nope, jnp.broadcast_to(k_rope, (B, S, MLA_HEADS, MLA_ROPE))], axis=-1)
        y_b = causal_attention(q_m.transpose(0, 2, 1, 3), k_m.transpose(0, 2, 1, 3), v_m.transpose(0, 2, 1, 3))

        merged = (jax.nn.sigmoid(g_a + b_gates[i, 0]) * (y_a @ w_branch_moba[i])
                  + jax.nn.sigmoid(g_b + b_gates[i, 1]) * (y_b @ w_branch_mla[i]))
        mix = merged @ w_out[i]
        h = layer_norm(DEEPNORM_ALPHA * x + mix, ln1_g[i], ln1_b[i])

        r = DEEPNORM_ALPHA * h + peer_ffn(h, peer_w_q[i], peer_sub_keys[i], peer_u[i], peer_v[i])
        ple = jax.nn.sigmoid(r @ w_ple_gate[i] + b_ple_gate[i]) * (p[i] @ w_ple_proj[i])
        x = layer_norm(r + ple, ln2_g[i], ln2_b[i])
    return x
```

```python
import functools
import math

import jax
import jax.numpy as jnp
from jax import lax
from jax.experimental import pallas as pl
from jax.experimental.pallas import tpu as pltpu

F32 = jnp.float32
BF16 = jnp.bfloat16

MOBA_HEADS = 8
MOBA_HEAD_DIM = 64
MOBA_BLOCK = 256
MOBA_TOPK = 3
MOBA_ROT = MOBA_HEAD_DIM // 4
MOBA_THETA = 500000.0
MOBA_W = MOBA_HEADS * MOBA_HEAD_DIM
MLA_HEADS = 8
MLA_Q_LORA = 512
MLA_KV_LORA = 256
MLA_NOPE = 64
MLA_ROPE = 32
MLA_V = 64
MLA_QK = MLA_NOPE + MLA_ROPE
MLA_THETA = 10000.0
PEER_HEADS = 8
PEER_N_KEYS = 128
PEER_KEY_DIM = 256
PEER_TOPK = 16
NEG_INF = -1e30
LN_EPS = 1e-5
RMS_EPS = 1e-6

V7X_VMEM_LIMIT_BYTES = 48 * 1024 * 1024

TOKEN_TILE = 512
MLA_TILE = 512
PEER_SEL_TILE = 256
PEER_TOKEN_TILE = 512
PEER_EXPERT_TILE = 512

_PEER_CAND = tuple((r, s) for r in range(PEER_TOPK) for s in range(PEER_TOPK)
                   if (r + 1) * (s + 1) <= PEER_TOPK)
_PEER_CAND_ROWS = -(-len(_PEER_CAND) // 8) * 8


def _cparams(*sem):
    return pltpu.CompilerParams(dimension_semantics=sem, vmem_limit_bytes=V7X_VMEM_LIMIT_BYTES)


def _nt_dot(a, b, precision=None):
    return lax.dot_general(a, b, (((1,), (1,)), ((), ())), preferred_element_type=F32,
                           precision=precision)


def _dot(a, b):
    return jnp.dot(a, b, preferred_element_type=F32)


def _layer_norm(v, g, b):
    mu = jnp.mean(v, axis=-1, keepdims=True)
    d = v - mu
    var = jnp.mean(d * d, axis=-1, keepdims=True)
    return d * lax.rsqrt(var + LN_EPS) * g + b


def _rms_norm(v, g):
    return v * lax.rsqrt(jnp.mean(v * v, axis=-1, keepdims=True) + RMS_EPS) * g


def _sigmoid(v):
    return 1.0 / (1.0 + jnp.exp(-v))


def _moba_proj_kernel(x_ref, w_ref, cos_ref, sin_ref, q_ref, k_ref, v_ref, km_ref):
    xb = x_ref[...].astype(BF16)
    cos = cos_ref[...]
    sin = sin_ref[...]
    W = MOBA_W
    q = _dot(xb, w_ref[:, 0:W]) * cos + _dot(xb, w_ref[:, W:2 * W]) * sin
    k = _dot(xb, w_ref[:, 2 * W:3 * W]) * cos + _dot(xb, w_ref[:, 3 * W:4 * W]) * sin
    v = _dot(xb, w_ref[:, 4 * W:5 * W])
    q_ref[...] = q
    k_ref[...] = k.astype(BF16)
    v_ref[...] = v.astype(BF16)
    nblk = k.shape[0] // MOBA_BLOCK
    km_ref[0] = jnp.sum(k.reshape(nblk, MOBA_BLOCK, W), axis=1) * (1.0 / MOBA_BLOCK)


def _moba_proj(x2, w, cos, sin, seq):
    T, D = x2.shape
    tm = TOKEN_TILE
    nblk = tm // MOBA_BLOCK
    ps = seq // tm
    row = lambda i: (i, 0)
    pos = lambda i: (i % ps, 0)
    return pl.pallas_call(
        _moba_proj_kernel,
        grid=(T // tm,),
        in_specs=[pl.BlockSpec((tm, D), row),
                  pl.BlockSpec(w.shape, lambda i: (0, 0)),
                  pl.BlockSpec((tm, MOBA_W), pos),
                  pl.BlockSpec((tm, MOBA_W), pos)],
        out_specs=[pl.BlockSpec((tm, MOBA_W), row),
                   pl.BlockSpec((tm, MOBA_W), row),
                   pl.BlockSpec((tm, MOBA_W), row),
                   pl.BlockSpec((1, nblk, MOBA_W), lambda i: (i, 0, 0))],
        out_shape=[jax.ShapeDtypeStruct((T, MOBA_W), F32),
                   jax.ShapeDtypeStruct((T, MOBA_W), BF16),
                   jax.ShapeDtypeStruct((T, MOBA_W), BF16),
                   jax.ShapeDtypeStruct((T // tm, nblk, MOBA_W), F32)],
        compiler_params=_cparams("parallel"),
        name="moba_proj",
    )(x2, w, cos, sin)


def _mla_proj_kernel(x_ref, wcq_ref, wckv_ref, wkr_ref, gq_ref, gkv_ref, wqn_ref, wqr_ref, wqrr_ref,
                     wkn_ref, wv_ref, cos_ref, sin_ref,
                     qn_ref, qr_ref, kn_ref, v_ref, kr_ref):
    xb = x_ref[...].astype(BF16)
    cos = cos_ref[...]
    sin = sin_ref[...]
    scale = MLA_QK ** -0.5
    cq = _rms_norm(_dot(xb, wcq_ref[...]), gq_ref[...]).astype(BF16)
    qn_ref[...] = (_dot(cq, wqn_ref[...]) * scale).astype(BF16)
    qr = _dot(cq, wqr_ref[...]) * cos + _dot(cq, wqrr_ref[...]) * sin
    qr_ref[...] = (qr * scale).astype(BF16)
    ckv = _rms_norm(_dot(xb, wckv_ref[...]), gkv_ref[...]).astype(BF16)
    kn_ref[...] = _dot(ckv, wkn_ref[...]).astype(BF16)
    v_ref[...] = _dot(ckv, wv_ref[...]).astype(BF16)
    kr2 = _dot(xb, wkr_ref[...])
    kr = kr2[:, :MLA_ROPE] * cos[:, :MLA_ROPE] + kr2[:, MLA_ROPE:] * sin[:, :MLA_ROPE]
    kr_ref[...] = kr.astype(BF16)


def _mla_proj(x2, wcq, wckv, wkr2, gq, gkv, wqn, wqr, wqrr, wkn, wv, cos, sin, seq):
    T, D = x2.shape
    tm = TOKEN_TILE
    ps = seq // tm
    row = lambda i: (i, 0)
    pos = lambda i: (i % ps, 0)
    full = lambda a: pl.BlockSpec(a.shape, lambda i: (0, 0))
    HN = MLA_HEADS * MLA_NOPE
    HR = MLA_HEADS * MLA_ROPE
    HV = MLA_HEADS * MLA_V
    return pl.pallas_call(
        _mla_proj_kernel,
        grid=(T // tm,),
        in_specs=[pl.BlockSpec((tm, D), row), full(wcq), full(wckv), full(wkr2), full(gq), full(gkv),
                  full(wqn), full(wqr), full(wqrr), full(wkn), full(wv),
                  pl.BlockSpec((tm, HR), pos), pl.BlockSpec((tm, HR), pos)],
        out_specs=[pl.BlockSpec((tm, HN), row), pl.BlockSpec((tm, HR), row), pl.BlockSpec((tm, HN), row),
                   pl.BlockSpec((tm, HV), row), pl.BlockSpec((tm, MLA_ROPE), row)],
        out_shape=[jax.ShapeDtypeStruct((T, HN), BF16), jax.ShapeDtypeStruct((T, HR), BF16),
                   jax.ShapeDtypeStruct((T, HN), BF16), jax.ShapeDtypeStruct((T, HV), BF16),
                   jax.ShapeDtypeStruct((T, MLA_ROPE), BF16)],
        compiler_params=_cparams("parallel"),
        name="mla_proj",
    )(x2, wcq, wckv, wkr2, gq, gkv, wqn, wqr, wqrr, wkn, wv, cos, sin)


def _online_softmax_step(sT, vT, m, l, acc):
    m_new = jnp.maximum(m, jnp.max(sT, axis=0, keepdims=True))
    p = jnp.exp(sT - m_new)
    alpha = jnp.exp(m - m_new)
    l_new = alpha * l + jnp.sum(p, axis=0, keepdims=True)
    acc_new = alpha * acc + _dot(vT, p.astype(BF16))
    return m_new, l_new, acc_new


def _causal_bias(t):
    kpos = lax.broadcasted_iota(jnp.int32, (t, t), 0)
    qpos = lax.broadcasted_iota(jnp.int32, (t, t), 1)
    return jnp.where(kpos <= qpos, 0.0, NEG_INF).astype(F32)


def _moba_attn_kernel(q_ref, k_ref, vT_ref, km_ref, o_ref, sel_ref):
    t = MOBA_BLOCK
    qi = pl.program_id(2)
    nb = km_ref.shape[2]
    q = q_ref[0, 0]
    gate = _nt_dot(km_ref[0, 0], q, precision=lax.Precision.HIGHEST)
    blk = lax.broadcasted_iota(jnp.int32, (nb, t), 0)
    valid = blk < qi
    gate = jnp.where(valid, gate, NEG_INF)
    rank = jnp.zeros((nb, t), F32)
    for n in range(nb):
        row = gate[n:n + 1, :]
        before = jnp.logical_or(row > gate, jnp.logical_and(row == gate, blk > n))
        rank = rank + jnp.where(before, 1.0, 0.0)
    chosen = jnp.logical_and(valid, rank < float(MOBA_TOPK))
    sel_ref[...] = jnp.where(chosen, 0.0, NEG_INF)

    qb = (q * (MOBA_HEAD_DIM ** -0.5)).astype(BF16)
    dv = vT_ref.shape[2]
    q0 = pl.multiple_of(qi * t, t)
    s_own = _nt_dot(k_ref[0, 0, pl.ds(q0, t), :], qb) + _causal_bias(t)
    m0 = jnp.full((1, t), NEG_INF, F32)
    l0 = jnp.zeros((1, t), F32)
    a0 = jnp.zeros((dv, t), F32)
    carry = _online_softmax_step(s_own, vT_ref[0, 0, :, pl.ds(q0, t)], m0, l0, a0)

    def body(kj, c):
        k0 = pl.multiple_of(kj * t, t)
        sT = _nt_dot(k_ref[0, 0, pl.ds(k0, t), :], qb) + sel_ref[pl.ds(kj, 1), :]
        return _online_softmax_step(sT, vT_ref[0, 0, :, pl.ds(k0, t)], *c)

    m, l, acc = lax.fori_loop(0, qi, body, carry)
    o_ref[0, 0] = (acc / l).astype(o_ref.dtype)


def _moba_attn(q, k, vT, kmean):
    B, H, S, dh = q.shape
    t = MOBA_BLOCK
    nb = S // t
    return pl.pallas_call(
        _moba_attn_kernel,
        grid=(B, H, nb),
        in_specs=[pl.BlockSpec((1, 1, t, dh), lambda b, h, i: (b, h, i, 0)),
                  pl.BlockSpec((1, 1, S, dh), lambda b, h, i: (b, h, 0, 0)),
                  pl.BlockSpec((1, 1, dh, S), lambda b, h, i: (b, h, 0, 0)),
                  pl.BlockSpec((1, 1, nb, dh), lambda b, h, i: (b, h, 0, 0))],
        out_specs=pl.BlockSpec((1, 1, dh, t), lambda b, h, i: (b, h, 0, i)),
        out_shape=jax.ShapeDtypeStruct((B, H, dh, S), BF16),
        scratch_shapes=[pltpu.VMEM((nb, t), F32)],
        compiler_params=_cparams("parallel", "parallel", "arbitrary"),
        name="moba_attn",
    )(q, k, vT, kmean)


def _mla_attn_kernel(q_ref, k_ref, vT_ref, o_ref):
    t = MLA_TILE
    qi = pl.program_id(2)
    qb = q_ref[0, 0]
    dv = vT_ref.shape[2]
    q0 = pl.multiple_of(qi * t, t)
    s_own = _nt_dot(k_ref[0, 0, pl.ds(q0, t), :], qb) + _causal_bias(t)
    m0 = jnp.full((1, t), NEG_INF, F32)
    l0 = jnp.zeros((1, t), F32)
    a0 = jnp.zeros((dv, t), F32)
    carry = _online_softmax_step(s_own, vT_ref[0, 0, :, pl.ds(q0, t)], m0, l0, a0)

    def body(kj, c):
        k0 = pl.multiple_of(kj * t, t)
        sT = _nt_dot(k_ref[0, 0, pl.ds(k0, t), :], qb)
        return _online_softmax_step(sT, vT_ref[0, 0, :, pl.ds(k0, t)], *c)

    m, l, acc = lax.fori_loop(0, qi, body, carry)
    o_ref[0, 0] = (acc / l).astype(o_ref.dtype)


def _mla_attn(q, k, vT):
    B, H, S, dk = q.shape
    dv = vT.shape[2]
    t = MLA_TILE
    return pl.pallas_call(
        _mla_attn_kernel,
        grid=(B, H, S // t),
        in_specs=[pl.BlockSpec((1, 1, t, dk), lambda b, h, i: (b, h, i, 0)),
                  pl.BlockSpec((1, 1, S, dk), lambda b, h, i: (b, h, 0, 0)),
                  pl.BlockSpec((1, 1, dv, S), lambda b, h, i: (b, h, 0, 0))],
        out_specs=pl.BlockSpec((1, 1, dv, t), lambda b, h, i: (b, h, 0, i)),
        out_shape=jax.ShapeDtypeStruct((B, H, dv, S), BF16),
        compiler_params=_cparams("parallel", "parallel", "arbitrary"),
        name="mla_attn",
    )(q, k, vT)


def _merge_kernel(x_ref, ya_ref, yb_ref, wg_ref, bg_ref, wa_ref, wb_ref, wo_ref, g_ref, b_ref,
                  h_ref, hb_ref, *, alpha):
    x = x_ref[...]
    D = x.shape[1]
    xb = x.astype(BF16)
    ga = _sigmoid(_dot(xb, wg_ref[:, :D]) + bg_ref[:, :D])
    gb = _sigmoid(_dot(xb, wg_ref[:, D:]) + bg_ref[:, D:])
    merged = ga * _dot(ya_ref[...], wa_ref[...]) + gb * _dot(yb_ref[...], wb_ref[...])
    mix = _dot(merged.astype(BF16), wo_ref[...])
    h = _layer_norm(alpha * x + mix, g_ref[...], b_ref[...])
    h_ref[...] = h
    hb_ref[...] = h.astype(BF16)


def _merge(x2, ya, yb, wg, bg, wa, wb, wo, g, b, alpha):
    T, D = x2.shape
    tm = TOKEN_TILE
    row = lambda i: (i, 0)
    full = lambda a: pl.BlockSpec(a.shape, lambda i: (0, 0))
    return pl.pallas_call(
        functools.partial(_merge_kernel, alpha=alpha),
        grid=(T // tm,),
        in_specs=[pl.BlockSpec((tm, D), row), pl.BlockSpec((tm, ya.shape[1]), row),
                  pl.BlockSpec((tm, yb.shape[1]), row), full(wg), full(bg), full(wa), full(wb), full(wo),
                  full(g), full(b)],
        out_specs=[pl.BlockSpec((tm, D), row), pl.BlockSpec((tm, D), row)],
        out_shape=[jax.ShapeDtypeStruct((T, D), F32), jax.ShapeDtypeStruct((T, D), BF16)],
        compiler_params=_cparams("parallel"),
        name="merge_ln1",
    )(x2, ya, yb, wg, bg, wa, wb, wo, g, b)


def _peer_scores_kernel(h_ref, wq_ref, keys_ref, s_ref):
    q = _dot(h_ref[...], wq_ref[...]).astype(BF16)
    half = PEER_KEY_DIM // 2
    for g in range(2 * PEER_HEADS):
        s_ref[g] = _nt_dot(keys_ref[g], q[:, g * half:(g + 1) * half])


def _peer_scores(hb, wq, keys):
    T, D = hb.shape
    tm = TOKEN_TILE
    G = 2 * PEER_HEADS
    return pl.pallas_call(
        _peer_scores_kernel,
        grid=(T // tm,),
        in_specs=[pl.BlockSpec((tm, D), lambda i: (i, 0)),
                  pl.BlockSpec(wq.shape, lambda i: (0, 0)),
                  pl.BlockSpec(keys.shape, lambda i: (0, 0, 0))],
        out_specs=pl.BlockSpec((G, PEER_N_KEYS, tm), lambda i: (0, 0, i)),
        out_shape=jax.ShapeDtypeStruct((G, PEER_N_KEYS, T), F32),
        compiler_params=_cparams("parallel"),
        name="peer_scores",
    )(hb, wq, keys)


def _top_values(v, k):
    out = []
    for _ in range(k):
        m = jnp.max(v, axis=0, keepdims=True)
        out.append(m)
        v = jnp.where(v == m, -jnp.inf, v)
    return out


def _peer_select_kernel(s_ref, thr_ref, ca_ref, eb_ref, cand_ref):
    tl = s_ref.shape[2]
    cand_ref[...] = jnp.full(cand_ref.shape, -jnp.inf, F32)
    for h in range(PEER_HEADS):
        a = s_ref[2 * h]
        b = s_ref[2 * h + 1]
        ta = _top_values(a, PEER_TOPK)
        tb = _top_values(b, PEER_TOPK)
        for c, (r, s) in enumerate(_PEER_CAND):
            cand_ref[pl.ds(c, 1), :] = ta[r] + tb[s]
        cand = cand_ref[...]
        tau = jnp.full((1, tl), jnp.inf, F32)
        for c in range(len(_PEER_CAND)):
            row = cand[c:c + 1, :]
            larger = jnp.sum(jnp.where(cand > row, 1.0, 0.0), axis=0, keepdims=True)
            tau = jnp.where(larger < float(PEER_TOPK), jnp.minimum(tau, row), tau)
        top = ta[0] + tb[0]
        z = jnp.sum(jnp.where(cand >= tau, jnp.exp(cand - top), 0.0), axis=0, keepdims=True)
        thr = jnp.full(a.shape, jnp.inf, F32)
        for s in range(PEER_TOPK):
            thr = jnp.where(a + tb[s] >= tau, jnp.minimum(thr, tb[s]), thr)
        thr_ref[h] = thr
        ca_ref[h] = jnp.exp(a - ta[0]) / z
        eb_ref[h] = jnp.exp(b - tb[0])


def _peer_select(sT):
    G, N, T = sT.shape
    tl = PEER_SEL_TILE
    out = jax.ShapeDtypeStruct((PEER_HEADS, N, T), F32)
    ospec = pl.BlockSpec((PEER_HEADS, N, tl), lambda i: (0, 0, i))
    return pl.pallas_call(
        _peer_select_kernel,
        grid=(T // tl,),
        in_specs=[pl.BlockSpec((G, N, tl), lambda i: (0, 0, i))],
        out_specs=[ospec, ospec, ospec],
        out_shape=[out, out, out],
        scratch_shapes=[pltpu.VMEM((_PEER_CAND_ROWS, tl), F32)],
        compiler_params=_cparams("parallel"),
        name="peer_select",
    )(sT)


def _gelu(v):
    return 0.5 * v * (1.0 + lax.erf(v * (2.0 ** -0.5)))


def _peer_main_kernel(hT_ref, u_ref, vT_ref, b_ref, thr_ref, ca_ref, eb_ref, o_ref):
    e = pl.program_id(1)
    N = PEER_N_KEYS
    per = PEER_EXPERT_TILE // N

    @pl.when(e == 0)
    def _():
        o_ref[...] = jnp.zeros(o_ref.shape, o_ref.dtype)

    act = _gelu(_dot(u_ref[...], hT_ref[...]))
    parts = []
    for sub in range(per):
        i1 = e * per + sub
        w = jnp.zeros((N, hT_ref.shape[1]), F32)
        for h in range(PEER_HEADS):
            thr = thr_ref[h, pl.ds(i1, 1), :]
            ca = ca_ref[h, pl.ds(i1, 1), :]
            w = w + jnp.where(b_ref[h, 0] >= thr, eb_ref[h], 0.0) * ca
        parts.append((w * act[sub * N:(sub + 1) * N, :]).astype(BF16))
    wg = jnp.concatenate(parts, axis=0)
    o_ref[...] += _dot(vT_ref[...], wg)


def _peer_main(hT, u, vT, sT, thr, ca, eb):
    D, T = hT.shape
    NE = u.shape[0]
    tm = PEER_TOKEN_TILE
    te = PEER_EXPERT_TILE
    N = PEER_N_KEYS
    s4 = sT.reshape(PEER_HEADS, 2, N, T)
    stat = pl.BlockSpec((PEER_HEADS, N, tm), lambda i, e: (0, 0, i))
    return pl.pallas_call(
        _peer_main_kernel,
        grid=(T // tm, NE // te),
        in_specs=[pl.BlockSpec((D, tm), lambda i, e: (0, i)),
                  pl.BlockSpec((te, D), lambda i, e: (e, 0)),
                  pl.BlockSpec((D, te), lambda i, e: (0, e)),
                  pl.BlockSpec((PEER_HEADS, 1, N, tm), lambda i, e: (0, 1, 0, i)),
                  stat, stat, stat],
        out_specs=pl.BlockSpec((D, tm), lambda i, e: (0, i)),
        out_shape=jax.ShapeDtypeStruct((D, T), F32),
        compiler_params=_cparams("parallel", "arbitrary"),
        name="peer_main",
    )(hT, u, vT, s4, thr, ca, eb)


def _final_kernel(h_ref, peer_ref, p_ref, wg_ref, bg_ref, wp_ref, g_ref, b_ref, o_ref, *, alpha):
    r = alpha * h_ref[...] + peer_ref[...]
    gate = _sigmoid(_dot(r.astype(BF16), wg_ref[...]) + bg_ref[...])
    ple = gate * _dot(p_ref[...].astype(BF16), wp_ref[...])
    o_ref[...] = _layer_norm(r + ple, g_ref[...], b_ref[...])


def _final(h, peer, p2, wg, bg, wp, g, b, alpha):
    T, D = h.shape
    tm = TOKEN_TILE
    row = lambda i: (i, 0)
    full = lambda a: pl.BlockSpec(a.shape, lambda i: (0, 0))
    return pl.pallas_call(
        functools.partial(_final_kernel, alpha=alpha),
        grid=(T // tm,),
        in_specs=[pl.BlockSpec((tm, D), row), pl.BlockSpec((tm, D), row),
                  pl.BlockSpec((tm, p2.shape[1]), row), full(wg), full(bg), full(wp), full(g), full(b)],
        out_specs=pl.BlockSpec((tm, D), row),
        out_shape=jax.ShapeDtypeStruct((T, D), F32),
        compiler_params=_cparams("parallel"),
        name="ple_ln2",
    )(h, peer, p2, wg, bg, wp, g, b)


def _rope_tables(seq, rot_dim, theta):
    inv = jnp.power(jnp.float32(theta), -jnp.arange(0, rot_dim, 2, dtype=F32) / rot_dim)
    ang = jnp.arange(seq).astype(F32)[:, None] * inv[None, :]
    ang = jnp.concatenate([ang, ang], axis=-1)
    return jnp.cos(ang), jnp.sin(ang)


def _rotate_half_cols(w):
    r = w.shape[-1]
    return jnp.concatenate([-w[..., r // 2:], w[..., :r // 2]], axis=-1)


def _head_split(a, B, S, H, d):
    return a.reshape(B, S, H, d).transpose(0, 2, 1, 3)


def _head_split_t(a, B, S, H, d):
    return a.reshape(B, S, H, d).transpose(0, 2, 3, 1)


def _head_merge_t(a):
    B, H, d, S = a.shape
    return a.transpose(0, 3, 1, 2).reshape(B * S, H * d)


def kernel(x, p, w_in, b_gates, mla_q_norm, w_q_up, mla_kv_norm, w_kv_up, w_branch_moba, w_branch_mla, w_out, ln1_g, ln1_b, peer_w_q, peer_sub_keys, peer_u, peer_v, w_ple_proj, w_ple_gate, b_ple_gate, ln2_g, ln2_b):
    B, S, D = x.shape
    depth = w_in.shape[0]
    T = B * S
    alpha = (2.0 * depth) ** 0.25
    H, dh = MOBA_HEADS, MOBA_HEAD_DIM

    cos_p, sin_p = _rope_tables(S, MOBA_ROT, MOBA_THETA)
    cos_a = jnp.tile(jnp.concatenate([cos_p, jnp.ones((S, dh - MOBA_ROT), F32)], axis=1), (1, H))
    sin_a = jnp.tile(jnp.concatenate([sin_p, jnp.zeros((S, dh - MOBA_ROT), F32)], axis=1), (1, H))
    cos_m, sin_m = _rope_tables(S, MLA_ROPE, MLA_THETA)
    cos_b = jnp.tile(cos_m, (1, MLA_HEADS))
    sin_b = jnp.tile(sin_m, (1, MLA_HEADS))

    o_k, o_v, o_cq = MOBA_W, 2 * MOBA_W, 3 * MOBA_W
    o_ckv = o_cq + MLA_Q_LORA
    o_kr = o_ckv + MLA_KV_LORA
    o_g = o_kr + MLA_ROPE

    def moba_rot(w):
        w3 = w.reshape(D, H, dh)
        rot = _rotate_half_cols(w3[..., :MOBA_ROT])
        return jnp.concatenate([rot, jnp.zeros((D, H, dh - MOBA_ROT), w.dtype)], axis=-1).reshape(D, H * dh)

    x2 = x.reshape(T, D)
    for i in range(depth):
        w = w_in[i]
        wq, wk, wv = w[:, :o_k], w[:, o_k:o_v], w[:, o_v:o_cq]
        w_moba = jnp.concatenate([wq, moba_rot(wq), wk, moba_rot(wk), wv], axis=1).astype(BF16)
        q_a, k_a, v_a, kmean = _moba_proj(x2, w_moba, cos_a, sin_a, S)

        wkr = w[:, o_kr:o_g]
        wkr2 = jnp.concatenate([wkr, _rotate_half_cols(wkr)], axis=1).astype(BF16)
        wqu = w_q_up[i].reshape(MLA_Q_LORA, MLA_HEADS, MLA_QK)
        wqn = wqu[..., :MLA_NOPE].reshape(MLA_Q_LORA, -1).astype(BF16)
        wqr = wqu[..., MLA_NOPE:]
        wqrr = _rotate_half_cols(wqr).reshape(MLA_Q_LORA, -1).astype(BF16)
        wqr = wqr.reshape(MLA_Q_LORA, -1).astype(BF16)
        wkv = w_kv_up[i].reshape(MLA_KV_LORA, MLA_HEADS, MLA_NOPE + MLA_V)
        wkn = wkv[..., :MLA_NOPE].reshape(MLA_KV_LORA, -1).astype(BF16)
        wvv = wkv[..., MLA_NOPE:].reshape(MLA_KV_LORA, -1).astype(BF16)
        qn, qr, kn, v_m, kr = _mla_proj(
            x2, w[:, o_cq:o_ckv].astype(BF16), w[:, o_ckv:o_kr].astype(BF16), wkr2,
            mla_q_norm[i][None, :], mla_kv_norm[i][None, :], wqn, wqr, wqrr, wkn, wvv, cos_b, sin_b, S)

        kmean = kmean.reshape(B, S // MOBA_BLOCK, H, dh).transpose(0, 2, 1, 3)
        y_a = _moba_attn(_head_split(q_a, B, S, H, dh), _head_split(k_a, B, S, H, dh),
                         _head_split_t(v_a, B, S, H, dh), kmean)
        y_a = _head_merge_t(y_a)

        q_m = jnp.concatenate([_head_split(qn, B, S, MLA_HEADS, MLA_NOPE),
                               _head_split(qr, B, S, MLA_HEADS, MLA_ROPE)], axis=-1)
        k_rope = jnp.broadcast_to(kr.reshape(B, 1, S, MLA_ROPE), (B, MLA_HEADS, S, MLA_ROPE))
        k_m = jnp.concatenate([_head_split(kn, B, S, MLA_HEADS, MLA_NOPE), k_rope], axis=-1)
        y_b = _mla_attn(q_m, k_m, _head_split_t(v_m, B, S, MLA_HEADS, MLA_V))
        y_b = _head_merge_t(y_b)

        h, hb = _merge(x2, y_a, y_b, w[:, o_g:].astype(BF16), b_gates[i].reshape(1, -1),
                       w_branch_moba[i].astype(BF16), w_branch_mla[i].astype(BF16), w_out[i].astype(BF16),
                       ln1_g[i][None, :], ln1_b[i][None, :], alpha)

        keys = peer_sub_keys[i].reshape(2 * PEER_HEADS, PEER_N_KEYS, PEER_KEY_DIM // 2).astype(BF16)
        sT = _peer_scores(hb, peer_w_q[i].astype(BF16), keys)
        thr, ca, eb = _peer_select(sT)
        peer_t = _peer_main(hb.T, peer_u[i].astype(BF16), peer_v[i].T.astype(BF16), sT, thr, ca, eb)

        x2 = _final(h, peer_t.T, p[i].reshape(T, -1), w_ple_gate[i].astype(BF16), b_ple_gate[i][None, :],
                    w_ple_proj[i].astype(BF16), ln2_g[i][None, :], ln2_b[i][None, :], alpha)
    return x2.reshape(B, S, D)
```

```python
import functools

import jax
import jax.numpy as jnp
from jax import lax
from jax.experimental import pallas as pl
from jax.experimental.pallas import tpu as pltpu

F32 = jnp.float32
BF16 = jnp.bfloat16

LANES = 128
MOBA_HEADS = 8
MOBA_HEAD_DIM = 64
MOBA_BLOCK = 256
MOBA_TOPK = 3
MOBA_ROT = MOBA_HEAD_DIM // 4
MOBA_THETA = 500000.0
MOBA_W = MOBA_HEADS * MOBA_HEAD_DIM
MLA_HEADS = 8
MLA_Q_LORA = 512
MLA_KV_LORA = 256
MLA_NOPE = 64
MLA_ROPE = 32
MLA_V = 64
MLA_QK = MLA_NOPE + MLA_ROPE
MLA_PAD = LANES
MLA_THETA = 10000.0
PEER_HEADS = 8
PEER_N_KEYS = 128
PEER_KEY_DIM = 256
PEER_TOPK = 16
NEG_INF = -1e30
LN_EPS = 1e-5
RMS_EPS = 1e-6

V7X_VMEM_LIMIT_BYTES = 48 * 1024 * 1024

TOKEN_TILE = 512
MOBA_GROUP = 4
MLA_TILE = 512
MLA_GROUP = 2
PEER_SEL_TILE = 256
PEER_TOKEN_TILE = 512
PEER_EXPERT_TILE = 512
PEER_EXPERT_HALF = 256

_PEER_CAND = tuple((r, s) for r in range(PEER_TOPK) for s in range(PEER_TOPK)
                   if (r + 1) * (s + 1) <= PEER_TOPK)
_PEER_CAND_ROWS = -(-len(_PEER_CAND) // 8) * 8


def _cparams(*sem):
    return pltpu.CompilerParams(dimension_semantics=sem, vmem_limit_bytes=V7X_VMEM_LIMIT_BYTES)


def _nt_dot(a, b, precision=None):
    return lax.dot_general(a, b, (((1,), (1,)), ((), ())), preferred_element_type=F32,
                           precision=precision)


def _dot(a, b):
    return jnp.dot(a, b, preferred_element_type=F32)


def _layer_norm(v, g, b):
    mu = jnp.mean(v, axis=-1, keepdims=True)
    d = v - mu
    var = jnp.mean(d * d, axis=-1, keepdims=True)
    return d * lax.rsqrt(var + LN_EPS) * g + b


def _rms_norm(v, g):
    return v * lax.rsqrt(jnp.mean(v * v, axis=-1, keepdims=True) + RMS_EPS) * g


def _sigmoid(v):
    return 1.0 / (1.0 + jnp.exp(-v))


def _tile_lanes(v, n):
    return jnp.concatenate([v] * n, axis=1)


def _moba_proj_kernel(x_ref, w_ref, cos_ref, sin_ref, q_ref, k_ref, v_ref, km_ref):
    xb = x_ref[...].astype(BF16)
    W = MOBA_W
    cos = _tile_lanes(cos_ref[...], W // LANES)
    sin = _tile_lanes(sin_ref[...], W // LANES)
    q = _dot(xb, w_ref[:, 0:W]) * cos + _dot(xb, w_ref[:, W:2 * W]) * sin
    k = _dot(xb, w_ref[:, 2 * W:3 * W]) * cos + _dot(xb, w_ref[:, 3 * W:4 * W]) * sin
    v = _dot(xb, w_ref[:, 4 * W:5 * W])
    q_ref[...] = q
    k_ref[...] = k.astype(BF16)
    v_ref[...] = v.astype(BF16)
    nblk = k.shape[0] // MOBA_BLOCK
    km_ref[0] = jnp.sum(k.reshape(nblk, MOBA_BLOCK, W), axis=1) * (1.0 / MOBA_BLOCK)


def _moba_proj(x2, w, cos, sin, seq):
    T, D = x2.shape
    tm = TOKEN_TILE
    nblk = tm // MOBA_BLOCK
    ps = seq // tm
    row = lambda i: (i, 0)
    pos = lambda i: (i % ps, 0)
    return pl.pallas_call(
        _moba_proj_kernel,
        grid=(T // tm,),
        in_specs=[pl.BlockSpec((tm, D), row),
                  pl.BlockSpec(w.shape, lambda i: (0, 0)),
                  pl.BlockSpec((tm, LANES), pos),
                  pl.BlockSpec((tm, LANES), pos)],
        out_specs=[pl.BlockSpec((tm, MOBA_W), row),
                   pl.BlockSpec((tm, MOBA_W), row),
                   pl.BlockSpec((tm, MOBA_W), row),
                   pl.BlockSpec((1, nblk, MOBA_W), lambda i: (i, 0, 0))],
        out_shape=[jax.ShapeDtypeStruct((T, MOBA_W), F32),
                   jax.ShapeDtypeStruct((T, MOBA_W), BF16),
                   jax.ShapeDtypeStruct((T, MOBA_W), BF16),
                   jax.ShapeDtypeStruct((T // tm, nblk, MOBA_W), F32)],
        compiler_params=_cparams("parallel"),
        name="moba_proj",
    )(x2, w, cos, sin)


def _mla_proj_kernel(x_ref, wcq_ref, wckv_ref, wkr_ref, gq_ref, gkv_ref, wq_ref, wqr_ref,
                     wkn_ref, wv_ref, cos_ref, sin_ref, qm_ref, km_ref, v_ref):
    xb = x_ref[...].astype(BF16)
    cos1 = cos_ref[...]
    sin1 = sin_ref[...]
    cos = _tile_lanes(cos1, MLA_HEADS)
    sin = _tile_lanes(sin1, MLA_HEADS)
    scale = MLA_QK ** -0.5
    cq = _rms_norm(_dot(xb, wcq_ref[...]), gq_ref[...]).astype(BF16)
    q = _dot(cq, wq_ref[...]) * cos + _dot(cq, wqr_ref[...]) * sin
    qm_ref[...] = (q * scale).astype(BF16)
    ckv = _rms_norm(_dot(xb, wckv_ref[...]), gkv_ref[...]).astype(BF16)
    kr2 = _dot(xb, wkr_ref[...])
    kr = kr2[:, :MLA_PAD] * cos1 + kr2[:, MLA_PAD:] * sin1
    km_ref[...] = (_dot(ckv, wkn_ref[...]) + _tile_lanes(kr, MLA_HEADS)).astype(BF16)
    v_ref[...] = _dot(ckv, wv_ref[...]).astype(BF16)


def _mla_proj(x2, wcq, wckv, wkr2, gq, gkv, wq, wqr, wkn, wv, cos, sin, seq):
    T, D = x2.shape
    tm = TOKEN_TILE
    ps = seq // tm
    row = lambda i: (i, 0)
    pos = lambda i: (i % ps, 0)
    full = lambda a: pl.BlockSpec(a.shape, lambda i: (0, 0))
    HP = MLA_HEADS * MLA_PAD
    HV = MLA_HEADS * MLA_V
    return pl.pallas_call(
        _mla_proj_kernel,
        grid=(T // tm,),
        in_specs=[pl.BlockSpec((tm, D), row), full(wcq), full(wckv), full(wkr2), full(gq), full(gkv),
                  full(wq), full(wqr), full(wkn), full(wv),
                  pl.BlockSpec((tm, MLA_PAD), pos), pl.BlockSpec((tm, MLA_PAD), pos)],
        out_specs=[pl.BlockSpec((tm, HP), row), pl.BlockSpec((tm, HP), row), pl.BlockSpec((tm, HV), row)],
        out_shape=[jax.ShapeDtypeStruct((T, HP), BF16), jax.ShapeDtypeStruct((T, HP), BF16),
                   jax.ShapeDtypeStruct((T, HV), BF16)],
        compiler_params=_cparams("parallel"),
        name="mla_proj",
    )(x2, wcq, wckv, wkr2, gq, gkv, wq, wqr, wkn, wv, cos, sin)


def _softmax_first(sT, vT, m_ref, l_ref, acc_ref, g, rows):
    m = jnp.max(sT, axis=0, keepdims=True)
    p = jnp.exp(sT - m)
    m_ref[g] = m
    l_ref[g] = jnp.sum(p, axis=0, keepdims=True)
    acc_ref[rows, :] = _dot(vT, p.astype(BF16))


def _softmax_update(sT, vT, m_ref, l_ref, acc_ref, g, rows):
    m_old = m_ref[g]
    m_new = jnp.maximum(m_old, jnp.max(sT, axis=0, keepdims=True))
    p = jnp.exp(sT - m_new)
    alpha = jnp.exp(m_old - m_new)
    m_ref[g] = m_new
    l_ref[g] = alpha * l_ref[g] + jnp.sum(p, axis=0, keepdims=True)
    acc_ref[rows, :] = alpha * acc_ref[rows, :] + _dot(vT, p.astype(BF16))


def _causal_bias(t):
    kpos = lax.broadcasted_iota(jnp.int32, (t, t), 0)
    qpos = lax.broadcasted_iota(jnp.int32, (t, t), 1)
    return jnp.where(kpos <= qpos, 0.0, NEG_INF).astype(F32)


def _attn_finish(o_ref, l_ref, acc_ref, groups, dv):
    for g in range(groups):
        rows = slice(g * dv, (g + 1) * dv)
        acc_ref[rows, :] = acc_ref[rows, :] / l_ref[g]
    o_ref[...] = acc_ref[...].T.astype(o_ref.dtype)


def _moba_attn_kernel(q_ref, k_ref, vT_ref, km_ref, o_ref, sel_ref, qb_ref, m_ref, l_ref, acc_ref):
    t = MOBA_BLOCK
    G = MOBA_GROUP
    dh = MOBA_HEAD_DIM
    qi = pl.program_id(2)
    nb = km_ref.shape[0]
    q_all = q_ref[...]
    lane = lax.broadcasted_iota(jnp.int32, q_all.shape, 1)
    blk = lax.broadcasted_iota(jnp.int32, (nb, t), 0)
    valid = blk < qi
    for g in range(G):
        qg = jnp.where(jnp.logical_and(lane >= g * dh, lane < (g + 1) * dh), q_all, 0.0)
        gate = _nt_dot(km_ref[...], qg, precision=lax.Precision.HIGHEST)
        gate = jnp.where(valid, gate, NEG_INF)
        rank = jnp.zeros((nb, t), F32)
        for n in range(nb):
            row = gate[n:n + 1, :]
            before = jnp.logical_or(row > gate, jnp.logical_and(row == gate, blk > n))
            rank = rank + jnp.where(before, 1.0, 0.0)
        chosen = jnp.logical_and(valid, rank < float(MOBA_TOPK))
        sel_ref[g] = jnp.where(chosen, 0.0, NEG_INF)
        qb_ref[g] = (qg * (dh ** -0.5)).astype(BF16)

    q0 = pl.multiple_of(qi * t, t)
    bias = _causal_bias(t)
    k_own = k_ref[pl.ds(q0, t), :]
    for g in range(G):
        _softmax_first(_nt_dot(k_own, qb_ref[g]) + bias, vT_ref[0, g, :, pl.ds(q0, t)],
                       m_ref, l_ref, acc_ref, g, slice(g * dh, (g + 1) * dh))

    def body(kj, c):
        k0 = pl.multiple_of(kj * t, t)
        kb = k_ref[pl.ds(k0, t), :]
        for g in range(G):
            sT = _nt_dot(kb, qb_ref[g]) + sel_ref[g, pl.ds(kj, 1), :]
            _softmax_update(sT, vT_ref[0, g, :, pl.ds(k0, t)], m_ref, l_ref, acc_ref, g,
                            slice(g * dh, (g + 1) * dh))
        return c

    lax.fori_loop(0, qi, body, 0)
    _attn_finish(o_ref, l_ref, acc_ref, G, dh)


def _moba_attn(q, k, vT, kmean, B, S):
    H, dh, G = MOBA_HEADS, MOBA_HEAD_DIM, MOBA_GROUP
    t = MOBA_BLOCK
    nb = S // t
    gw = G * dh
    return pl.pallas_call(
        _moba_attn_kernel,
        grid=(B, H // G, nb),
        in_specs=[pl.BlockSpec((t, gw), lambda b, h, i: (b * nb + i, h)),
                  pl.BlockSpec((S, gw), lambda b, h, i: (b, h)),
                  pl.BlockSpec((1, G, dh, S), lambda b, h, i: (b, h, 0, 0)),
                  pl.BlockSpec((nb, gw), lambda b, h, i: (b, h))],
        out_specs=pl.BlockSpec((t, gw), lambda b, h, i: (b * nb + i, h)),
        out_shape=jax.ShapeDtypeStruct((B * S, H * dh), BF16),
        scratch_shapes=[pltpu.VMEM((G, nb, t), F32), pltpu.VMEM((G, t, gw), BF16),
                        pltpu.VMEM((G, 1, t), F32), pltpu.VMEM((G, 1, t), F32),
                        pltpu.VMEM((gw, t), F32)],
        compiler_params=_cparams("parallel", "parallel", "arbitrary"),
        name="moba_attn",
    )(q, k, vT, kmean)


def _mla_attn_kernel(q_ref, k_ref, vT_ref, o_ref, m_ref, l_ref, acc_ref):
    t = MLA_TILE
    G = MLA_GROUP
    P = MLA_PAD
    dv = MLA_V
    qi = pl.program_id(2)
    q0 = pl.multiple_of(qi * t, t)
    bias = _causal_bias(t)
    for g in range(G):
        cols = slice(g * P, (g + 1) * P)
        sT = _nt_dot(k_ref[pl.ds(q0, t), cols], q_ref[:, cols]) + bias
        _softmax_first(sT, vT_ref[0, g, :, pl.ds(q0, t)], m_ref, l_ref, acc_ref, g,
                       slice(g * dv, (g + 1) * dv))

    def body(kj, c):
        k0 = pl.multiple_of(kj * t, t)
        for g in range(G):
            cols = slice(g * P, (g + 1) * P)
            sT = _nt_dot(k_ref[pl.ds(k0, t), cols], q_ref[:, cols])
            _softmax_update(sT, vT_ref[0, g, :, pl.ds(k0, t)], m_ref, l_ref, acc_ref, g,
                            slice(g * dv, (g + 1) * dv))
        return c

    lax.fori_loop(0, qi, body, 0)
    _attn_finish(o_ref, l_ref, acc_ref, G, dv)


def _mla_attn(q, k, vT, B, S):
    H, G, P, dv = MLA_HEADS, MLA_GROUP, MLA_PAD, MLA_V
    t = MLA_TILE
    nq = S // t
    return pl.pallas_call(
        _mla_attn_kernel,
        grid=(B, H // G, nq),
        in_specs=[pl.BlockSpec((t, G * P), lambda b, h, i: (b * nq + i, h)),
                  pl.BlockSpec((S, G * P), lambda b, h, i: (b, h)),
                  pl.BlockSpec((1, G, dv, S), lambda b, h, i: (b, h, 0, 0))],
        out_specs=pl.BlockSpec((t, G * dv), lambda b, h, i: (b * nq + i, h)),
        out_shape=jax.ShapeDtypeStruct((B * S, H * dv), BF16),
        scratch_shapes=[pltpu.VMEM((G, 1, t), F32), pltpu.VMEM((G, 1, t), F32),
                        pltpu.VMEM((G * dv, t), F32)],
        compiler_params=_cparams("parallel", "parallel", "arbitrary"),
        name="mla_attn",
    )(q, k, vT)


def _merge_kernel(x_ref, ya_ref, yb_ref, wg_ref, bg_ref, wa_ref, wb_ref, wo_ref, g_ref, b_ref,
                  h_ref, hb_ref, *, alpha):
    x = x_ref[...]
    D = x.shape[1]
    xb = x.astype(BF16)
    ga = _sigmoid(_dot(xb, wg_ref[:, :D]) + bg_ref[:, :D])
    gb = _sigmoid(_dot(xb, wg_ref[:, D:]) + bg_ref[:, D:])
    merged = ga * _dot(ya_ref[...], wa_ref[...]) + gb * _dot(yb_ref[...], wb_ref[...])
    mix = _dot(merged.astype(BF16), wo_ref[...])
    h = _layer_norm(alpha * x + mix, g_ref[...], b_ref[...])
    h_ref[...] = h
    hb_ref[...] = h.astype(BF16)


def _merge(x2, ya, yb, wg, bg, wa, wb, wo, g, b, alpha):
    T, D = x2.shape
    tm = TOKEN_TILE
    row = lambda i: (i, 0)
    full = lambda a: pl.BlockSpec(a.shape, lambda i: (0, 0))
    return pl.pallas_call(
        functools.partial(_merge_kernel, alpha=alpha),
        grid=(T // tm,),
        in_specs=[pl.BlockSpec((tm, D), row), pl.BlockSpec((tm, ya.shape[1]), row),
                  pl.BlockSpec((tm, yb.shape[1]), row), full(wg), full(bg), full(wa), full(wb), full(wo),
                  full(g), full(b)],
        out_specs=[pl.BlockSpec((tm, D), row), pl.BlockSpec((tm, D), row)],
        out_shape=[jax.ShapeDtypeStruct((T, D), F32), jax.ShapeDtypeStruct((T, D), BF16)],
        compiler_params=_cparams("parallel"),
        name="merge_ln1",
    )(x2, ya, yb, wg, bg, wa, wb, wo, g, b)


def _peer_scores_kernel(h_ref, wq_ref, keys_ref, s_ref):
    q = _dot(h_ref[...], wq_ref[...]).astype(BF16)
    half = PEER_KEY_DIM // 2
    for g in range(2 * PEER_HEADS):
        s_ref[g] = _nt_dot(keys_ref[g], q[:, g * half:(g + 1) * half])


def _peer_scores(hb, wq, keys):
    T, D = hb.shape
    tm = TOKEN_TILE
    G = 2 * PEER_HEADS
    return pl.pallas_call(
        _peer_scores_kernel,
        grid=(T // tm,),
        in_specs=[pl.BlockSpec((tm, D), lambda i: (i, 0)),
                  pl.BlockSpec(wq.shape, lambda i: (0, 0)),
                  pl.BlockSpec(keys.shape, lambda i: (0, 0, 0))],
        out_specs=pl.BlockSpec((G, PEER_N_KEYS, tm), lambda i: (0, 0, i)),
        out_shape=jax.ShapeDtypeStruct((G, PEER_N_KEYS, T), F32),
        compiler_params=_cparams("parallel"),
        name="peer_scores",
    )(hb, wq, keys)


def _top_values(v, k):
    out = []
    for _ in range(k):
        m = jnp.max(v, axis=0, keepdims=True)
        out.append(m)
        v = jnp.where(v == m, -jnp.inf, v)
    return out


def _peer_select_kernel(s_ref, cnt_ref, ca_ref, rb_ref, eb_ref, cand_ref):
    tl = s_ref.shape[2]
    cand_ref[...] = jnp.full(cand_ref.shape, -jnp.inf, F32)
    for h in range(PEER_HEADS):
        a = s_ref[2 * h]
        b = s_ref[2 * h + 1]
        ta = _top_values(a, PEER_TOPK)
        tb = _top_values(b, PEER_TOPK)
        for c, (r, s) in enumerate(_PEER_CAND):
            cand_ref[pl.ds(c, 1), :] = ta[r] + tb[s]
        cand = cand_ref[...]
        tau = jnp.full((1, tl), jnp.inf, F32)
        for c in range(len(_PEER_CAND)):
            row = cand[c:c + 1, :]
            larger = jnp.sum(jnp.where(cand > row, 1.0, 0.0), axis=0, keepdims=True)
            tau = jnp.where(larger < float(PEER_TOPK), jnp.minimum(tau, row), tau)
        top = ta[0] + tb[0]
        z = jnp.sum(jnp.where(cand >= tau, jnp.exp(cand - top), 0.0), axis=0, keepdims=True)
        cnt = jnp.zeros(a.shape, F32)
        rb = jnp.zeros(b.shape, F32)
        for s in range(PEER_TOPK):
            cnt = cnt + jnp.where(a + tb[s] >= tau, 1.0, 0.0)
            rb = rb + jnp.where(tb[s] > b, 1.0, 0.0)
        cnt_ref[h] = cnt
        ca_ref[h] = jnp.exp(a - ta[0]) / z
        rb_ref[h] = rb.astype(BF16)
        eb_ref[h] = jnp.exp(b - tb[0]).astype(BF16)


def _peer_select(sT):
    G, N, T = sT.shape
    tl = PEER_SEL_TILE
    ospec = pl.BlockSpec((PEER_HEADS, N, tl), lambda i: (0, 0, i))
    shape = (PEER_HEADS, N, T)
    return pl.pallas_call(
        _peer_select_kernel,
        grid=(T // tl,),
        in_specs=[pl.BlockSpec((G, N, tl), lambda i: (0, 0, i))],
        out_specs=[ospec, ospec, ospec, ospec],
        out_shape=[jax.ShapeDtypeStruct(shape, F32), jax.ShapeDtypeStruct(shape, F32),
                   jax.ShapeDtypeStruct(shape, BF16), jax.ShapeDtypeStruct(shape, BF16)],
        scratch_shapes=[pltpu.VMEM((_PEER_CAND_ROWS, tl), F32)],
        compiler_params=_cparams("parallel"),
        name="peer_select",
    )(sT)


def _gelu(v):
    return 0.5 * v * (1.0 + lax.erf(v * (2.0 ** -0.5)))


def _peer_main_kernel(hT_ref, u_ref, vT_ref, cnt_ref, ca_ref, rb_ref, eb_ref, o_ref):
    e = pl.program_id(1)
    N = PEER_N_KEYS
    EH = PEER_EXPERT_HALF
    halves = PEER_EXPERT_TILE // EH
    per = EH // N
    tm = hT_ref.shape[1]

    @pl.when(e == 0)
    def _():
        o_ref[...] = jnp.zeros(o_ref.shape, o_ref.dtype)

    hT = hT_ref[...]
    contrib = None
    for half in range(halves):
        rows = slice(half * EH, (half + 1) * EH)
        act = _gelu(_dot(u_ref[rows, :], hT))
        parts = []
        for sub in range(per):
            i1 = (e * halves + half) * per + sub
            w = jnp.zeros((N, tm), BF16)
            for h in range(PEER_HEADS):
                cnt = cnt_ref[h, pl.ds(i1, 1), :].astype(BF16)
                ca = ca_ref[h, pl.ds(i1, 1), :].astype(BF16)
                w = w + jnp.where(rb_ref[h] < cnt, eb_ref[h], jnp.zeros((), BF16)) * ca
            parts.append(w * act[sub * N:(sub + 1) * N, :].astype(BF16))
        d = _dot(vT_ref[:, rows], jnp.concatenate(parts, axis=0))
        contrib = d if contrib is None else contrib + d
    o_ref[...] += contrib


def _peer_main(hT, u, vT, cnt, ca, rb, eb):
    D, T = hT.shape
    NE = u.shape[0]
    tm = PEER_TOKEN_TILE
    te = PEER_EXPERT_TILE
    stat = pl.BlockSpec((PEER_HEADS, PEER_N_KEYS, tm), lambda i, e: (0, 0, i))
    return pl.pallas_call(
        _peer_main_kernel,
        grid=(T // tm, NE // te),
        in_specs=[pl.BlockSpec((D, tm), lambda i, e: (0, i)),
                  pl.BlockSpec((te, D), lambda i, e: (e, 0)),
                  pl.BlockSpec((D, te), lambda i, e: (0, e)),
                  stat, stat, stat, stat],
        out_specs=pl.BlockSpec((D, tm), lambda i, e: (0, i)),
        out_shape=jax.ShapeDtypeStruct((D, T), F32),
        compiler_params=_cparams("parallel", "arbitrary"),
        name="peer_main",
    )(hT, u, vT, cnt, ca, rb, eb)


def _final_kernel(h_ref, peer_ref, p_ref, wg_ref, bg_ref, wp_ref, g_ref, b_ref, o_ref, *, alpha):
    r = alpha * h_ref[...] + peer_ref[...]
    gate = _sigmoid(_dot(r.astype(BF16), wg_ref[...]) + bg_ref[...])
    ple = gate * _dot(p_ref[...].astype(BF16), wp_ref[...])
    o_ref[...] = _layer_norm(r + ple, g_ref[...], b_ref[...])


def _final(h, peer, p2, wg, bg, wp, g, b, alpha):
    T, D = h.shape
    tm = TOKEN_TILE
    row = lambda i: (i, 0)
    full = lambda a: pl.BlockSpec(a.shape, lambda i: (0, 0))
    return pl.pallas_call(
        functools.partial(_final_kernel, alpha=alpha),
        grid=(T // tm,),
        in_specs=[pl.BlockSpec((tm, D), row), pl.BlockSpec((tm, D), row),
                  pl.BlockSpec((tm, p2.shape[1]), row), full(wg), full(bg), full(wp), full(g), full(b)],
        out_specs=pl.BlockSpec((tm, D), row),
        out_shape=jax.ShapeDtypeStruct((T, D), F32),
        compiler_params=_cparams("parallel"),
        name="ple_ln2",
    )(h, peer, p2, wg, bg, wp, g, b)


def _rope_tables(seq, rot_dim, theta):
    inv = jnp.power(jnp.float32(theta), -jnp.arange(0, rot_dim, 2, dtype=F32) / rot_dim)
    ang = jnp.arange(seq).astype(F32)[:, None] * inv[None, :]
    ang = jnp.concatenate([ang, ang], axis=-1)
    return jnp.cos(ang), jnp.sin(ang)


def _rotate_half_cols(w):
    r = w.shape[-1]
    return jnp.concatenate([-w[..., r // 2:], w[..., :r // 2]], axis=-1)


def _pad_last(a, before, total):
    pad = [(0, 0)] * (a.ndim - 1) + [(before, total - before - a.shape[-1])]
    return jnp.pad(a, pad)


def _head_split_t(a, B, S, H, d):
    return a.reshape(B, S, H, d).transpose(0, 2, 3, 1)


def kernel(x, p, w_in, b_gates, mla_q_norm, w_q_up, mla_kv_norm, w_kv_up, w_branch_moba, w_branch_mla, w_out, ln1_g, ln1_b, peer_w_q, peer_sub_keys, peer_u, peer_v, w_ple_proj, w_ple_gate, b_ple_gate, ln2_g, ln2_b):
    B, S, D = x.shape
    depth = w_in.shape[0]
    T = B * S
    alpha = (2.0 * depth) ** 0.25
    H, dh = MOBA_HEADS, MOBA_HEAD_DIM

    cos_p, sin_p = _rope_tables(S, MOBA_ROT, MOBA_THETA)
    cos_a = jnp.tile(jnp.concatenate([cos_p, jnp.ones((S, dh - MOBA_ROT), F32)], axis=1), (1, LANES // dh))
    sin_a = jnp.tile(jnp.concatenate([sin_p, jnp.zeros((S, dh - MOBA_ROT), F32)], axis=1), (1, LANES // dh))
    cos_m, sin_m = _rope_tables(S, MLA_ROPE, MLA_THETA)
    cos_b = jnp.concatenate([jnp.ones((S, MLA_NOPE), F32), cos_m,
                             jnp.zeros((S, MLA_PAD - MLA_QK), F32)], axis=1)
    sin_b = _pad_last(sin_m, MLA_NOPE, MLA_PAD)

    o_k, o_v, o_cq = MOBA_W, 2 * MOBA_W, 3 * MOBA_W
    o_ckv = o_cq + MLA_Q_LORA
    o_kr = o_ckv + MLA_KV_LORA
    o_g = o_kr + MLA_ROPE

    def moba_rot(w):
        w3 = w.reshape(D, H, dh)
        return _pad_last(_rotate_half_cols(w3[..., :MOBA_ROT]), 0, dh).reshape(D, H * dh)

    x2 = x.reshape(T, D)
    for i in range(depth):
        w = w_in[i]
        wq, wk, wv = w[:, :o_k], w[:, o_k:o_v], w[:, o_v:o_cq]
        w_moba = jnp.concatenate([wq, moba_rot(wq), wk, moba_rot(wk), wv], axis=1).astype(BF16)
        q_a, k_a, v_a, kmean = _moba_proj(x2, w_moba, cos_a, sin_a, S)

        wkr = w[:, o_kr:o_g]
        wkr2 = jnp.concatenate([_pad_last(wkr, MLA_NOPE, MLA_PAD),
                                _pad_last(_rotate_half_cols(wkr), MLA_NOPE, MLA_PAD)], axis=1).astype(BF16)
        wqu = w_q_up[i].reshape(MLA_Q_LORA, MLA_HEADS, MLA_QK)
        wq_pad = _pad_last(wqu, 0, MLA_PAD).reshape(MLA_Q_LORA, -1).astype(BF16)
        wqr_pad = _pad_last(_rotate_half_cols(wqu[..., MLA_NOPE:]), MLA_NOPE, MLA_PAD)
        wqr_pad = wqr_pad.reshape(MLA_Q_LORA, -1).astype(BF16)
        wkv = w_kv_up[i].reshape(MLA_KV_LORA, MLA_HEADS, MLA_NOPE + MLA_V)
        wkn_pad = _pad_last(wkv[..., :MLA_NOPE], 0, MLA_PAD).reshape(MLA_KV_LORA, -1).astype(BF16)
        wvv = wkv[..., MLA_NOPE:].reshape(MLA_KV_LORA, -1).astype(BF16)
        q_m, k_m, v_m = _mla_proj(
            x2, w[:, o_cq:o_ckv].astype(BF16), w[:, o_ckv:o_kr].astype(BF16), wkr2,
            mla_q_norm[i][None, :], mla_kv_norm[i][None, :], wq_pad, wqr_pad, wkn_pad, wvv, cos_b, sin_b, S)

        y_a = _moba_attn(q_a, k_a, _head_split_t(v_a, B, S, H, dh), kmean.reshape(T // MOBA_BLOCK, MOBA_W), B, S)
        y_b = _mla_attn(q_m, k_m, _head_split_t(v_m, B, S, MLA_HEADS, MLA_V), B, S)

        h, hb = _merge(x2, y_a, y_b, w[:, o_g:].astype(BF16), b_gates[i].reshape(1, -1),
                       w_branch_moba[i].astype(BF16), w_branch_mla[i].astype(BF16), w_out[i].astype(BF16),
                       ln1_g[i][None, :], ln1_b[i][None, :], alpha)

        keys = peer_sub_keys[i].reshape(2 * PEER_HEADS, PEER_N_KEYS, PEER_KEY_DIM // 2).astype(BF16)
        sT = _peer_scores(hb, peer_w_q[i].astype(BF16), keys)
        cnt, ca, rb, eb = _peer_select(sT)
        peer_t = _peer_main(hb.T, peer_u[i].astype(BF16), peer_v[i].T.astype(BF16), cnt, ca, rb, eb)

        x2 = _final(h, peer_t.T, p[i].reshape(T, -1), w_ple_gate[i].astype(BF16), b_ple_gate[i][None, :],
                    w_ple_proj[i].astype(BF16), ln2_g[i][None, :], ln2_b[i][None, :], alpha)
    return x2.reshape(B, S, D)
```

```python
import functools

import jax
import jax.numpy as jnp
from jax import lax
from jax.experimental import pallas as pl
from jax.experimental.pallas import tpu as pltpu

F32 = jnp.float32
BF16 = jnp.bfloat16

LANES = 128
MOBA_HEADS = 8
MOBA_HEAD_DIM = 64
MOBA_BLOCK = 256
MOBA_TOPK = 3
MOBA_ROT = MOBA_HEAD_DIM // 4
MOBA_THETA = 500000.0
MOBA_W = MOBA_HEADS * MOBA_HEAD_DIM
MLA_HEADS = 8
MLA_Q_LORA = 512
MLA_KV_LORA = 256
MLA_NOPE = 64
MLA_ROPE = 32
MLA_V = 64
MLA_QK = MLA_NOPE + MLA_ROPE
MLA_PAD = LANES
MLA_THETA = 10000.0
PEER_HEADS = 8
PEER_N_KEYS = 128
PEER_KEY_DIM = 256
PEER_TOPK = 16
NEG_INF = -1e30
LOG2_E = 1.4426950408889634
LN_EPS = 1e-5
RMS_EPS = 1e-6

V7X_VMEM_LIMIT_BYTES = 48 * 1024 * 1024

TOKEN_TILE = 512
MOBA_GROUP = 4
MLA_TILE = 512
MLA_GROUP = 2
PEER_SEL_TILE = 256
PEER_TOKEN_TILE = 512
PEER_EXPERT_TILE = 1024
PEER_EXPERT_HALF = 256

_PEER_CAND = tuple((r, s) for r in range(PEER_TOPK) for s in range(PEER_TOPK)
                   if (r + 1) * (s + 1) <= PEER_TOPK)
_PEER_CAND_ROWS = -(-len(_PEER_CAND) // 8) * 8


def _cparams(*sem):
    return pltpu.CompilerParams(dimension_semantics=sem, vmem_limit_bytes=V7X_VMEM_LIMIT_BYTES)


def _nt_dot(a, b, precision=None):
    return lax.dot_general(a, b, (((1,), (1,)), ((), ())), preferred_element_type=F32,
                           precision=precision)


def _dot(a, b):
    return jnp.dot(a, b, preferred_element_type=F32)


def _layer_norm(v, g, b):
    mu = jnp.mean(v, axis=-1, keepdims=True)
    d = v - mu
    var = jnp.mean(d * d, axis=-1, keepdims=True)
    return d * lax.rsqrt(var + LN_EPS) * g + b


def _rms_norm(v, g):
    return v * lax.rsqrt(jnp.mean(v * v, axis=-1, keepdims=True) + RMS_EPS) * g


def _sigmoid(v):
    return 1.0 / (1.0 + jnp.exp(-v))


def _tile_lanes(v, n):
    return jnp.concatenate([v] * n, axis=1)


def _moba_proj_kernel(x_ref, w_ref, cos_ref, sin_ref, q_ref, k_ref, v_ref, km_ref):
    xb = x_ref[...].astype(BF16)
    W = MOBA_W
    cos = _tile_lanes(cos_ref[...], W // LANES)
    sin = _tile_lanes(sin_ref[...], W // LANES)
    q = _dot(xb, w_ref[:, 0:W]) * cos + _dot(xb, w_ref[:, W:2 * W]) * sin
    k = _dot(xb, w_ref[:, 2 * W:3 * W]) * cos + _dot(xb, w_ref[:, 3 * W:4 * W]) * sin
    v = _dot(xb, w_ref[:, 4 * W:5 * W])
    q_ref[...] = q
    k_ref[...] = k.astype(BF16)
    v_ref[...] = v.astype(BF16)
    nblk = k.shape[0] // MOBA_BLOCK
    km_ref[0] = jnp.sum(k.reshape(nblk, MOBA_BLOCK, W), axis=1) * (1.0 / MOBA_BLOCK)


def _moba_proj(x2, w, cos, sin, seq):
    T, D = x2.shape
    tm = TOKEN_TILE
    nblk = tm // MOBA_BLOCK
    ps = seq // tm
    row = lambda i: (i, 0)
    pos = lambda i: (i % ps, 0)
    return pl.pallas_call(
        _moba_proj_kernel,
        grid=(T // tm,),
        in_specs=[pl.BlockSpec((tm, D), row),
                  pl.BlockSpec(w.shape, lambda i: (0, 0)),
                  pl.BlockSpec((tm, LANES), pos),
                  pl.BlockSpec((tm, LANES), pos)],
        out_specs=[pl.BlockSpec((tm, MOBA_W), row),
                   pl.BlockSpec((tm, MOBA_W), row),
                   pl.BlockSpec((tm, MOBA_W), row),
                   pl.BlockSpec((1, nblk, MOBA_W), lambda i: (i, 0, 0))],
        out_shape=[jax.ShapeDtypeStruct((T, MOBA_W), F32),
                   jax.ShapeDtypeStruct((T, MOBA_W), BF16),
                   jax.ShapeDtypeStruct((T, MOBA_W), BF16),
                   jax.ShapeDtypeStruct((T // tm, nblk, MOBA_W), F32)],
        compiler_params=_cparams("parallel"),
        name="moba_proj",
    )(x2, w, cos, sin)


def _mla_proj_kernel(x_ref, wcq_ref, wckv_ref, wkr_ref, gq_ref, gkv_ref, wq_ref, wqr_ref,
                     wkn_ref, wv_ref, cos_ref, sin_ref, qm_ref, km_ref, v_ref):
    xb = x_ref[...].astype(BF16)
    cos1 = cos_ref[...]
    sin1 = sin_ref[...]
    cos = _tile_lanes(cos1, MLA_HEADS)
    sin = _tile_lanes(sin1, MLA_HEADS)
    scale = MLA_QK ** -0.5 * LOG2_E
    cq = _rms_norm(_dot(xb, wcq_ref[...]), gq_ref[...]).astype(BF16)
    q = _dot(cq, wq_ref[...]) * cos + _dot(cq, wqr_ref[...]) * sin
    qm_ref[...] = (q * scale).astype(BF16)
    ckv = _rms_norm(_dot(xb, wckv_ref[...]), gkv_ref[...]).astype(BF16)
    kr2 = _dot(xb, wkr_ref[...])
    kr = kr2[:, :MLA_PAD] * cos1 + kr2[:, MLA_PAD:] * sin1
    km_ref[...] = (_dot(ckv, wkn_ref[...]) + _tile_lanes(kr, MLA_HEADS)).astype(BF16)
    v_ref[...] = _dot(ckv, wv_ref[...]).astype(BF16)


def _mla_proj(x2, wcq, wckv, wkr2, gq, gkv, wq, wqr, wkn, wv, cos, sin, seq):
    T, D = x2.shape
    tm = TOKEN_TILE
    ps = seq // tm
    row = lambda i: (i, 0)
    pos = lambda i: (i % ps, 0)
    full = lambda a: pl.BlockSpec(a.shape, lambda i: (0, 0))
    HP = MLA_HEADS * MLA_PAD
    HV = MLA_HEADS * MLA_V
    return pl.pallas_call(
        _mla_proj_kernel,
        grid=(T // tm,),
        in_specs=[pl.BlockSpec((tm, D), row), full(wcq), full(wckv), full(wkr2), full(gq), full(gkv),
                  full(wq), full(wqr), full(wkn), full(wv),
                  pl.BlockSpec((tm, MLA_PAD), pos), pl.BlockSpec((tm, MLA_PAD), pos)],
        out_specs=[pl.BlockSpec((tm, HP), row), pl.BlockSpec((tm, HP), row), pl.BlockSpec((tm, HV), row)],
        out_shape=[jax.ShapeDtypeStruct((T, HP), BF16), jax.ShapeDtypeStruct((T, HP), BF16),
                   jax.ShapeDtypeStruct((T, HV), BF16)],
        compiler_params=_cparams("parallel"),
        name="mla_proj",
    )(x2, wcq, wckv, wkr2, gq, gkv, wq, wqr, wkn, wv, cos, sin)


def _softmax_update(sT, vT, m_ref, l_ref, acc_ref, g, rows):
    m_old = m_ref[g]
    m_new = jnp.maximum(m_old, jnp.max(sT, axis=0, keepdims=True))
    p = jnp.exp2(sT - m_new)
    alpha = jnp.exp2(m_old - m_new)
    m_ref[g] = m_new
    l_ref[g] = alpha * l_ref[g] + jnp.sum(p, axis=0, keepdims=True)
    acc_ref[rows, :] = alpha * acc_ref[rows, :] + _dot(vT, p.astype(BF16))


def _attn_pipeline(qi, t, groups, dv, score_own, score_past, value_block, s_ref, m_ref, l_ref, acc_ref):
    m_ref[...] = jnp.full(m_ref.shape, NEG_INF, F32)
    l_ref[...] = jnp.zeros(l_ref.shape, F32)
    acc_ref[...] = jnp.zeros(acc_ref.shape, F32)
    for g in range(groups):
        s_ref[0, g] = score_own(g)

    def consume(slot, blk):
        for g in range(groups):
            _softmax_update(s_ref[slot, g], value_block(blk, g), m_ref, l_ref, acc_ref, g,
                            slice(g * dv, (g + 1) * dv))

    def body(kj, c):
        slot = kj % 2
        for g in range(groups):
            s_ref[1 - slot, g] = score_past(kj, g)
        consume(slot, jnp.where(kj == 0, qi, kj - 1))
        return c

    lax.fori_loop(0, qi, body, 0)
    consume(qi % 2, jnp.maximum(qi - 1, 0))


def _attn_serial(qi, t, groups, dv, score_own, score_past, value_block, m_ref, l_ref, acc_ref):
    m_ref[...] = jnp.full(m_ref.shape, NEG_INF, F32)
    l_ref[...] = jnp.zeros(l_ref.shape, F32)
    acc_ref[...] = jnp.zeros(acc_ref.shape, F32)
    for g in range(groups):
        _softmax_update(score_own(g), value_block(qi, g), m_ref, l_ref, acc_ref, g, slice(g * dv, (g + 1) * dv))

    def body(kj, c):
        for g in range(groups):
            _softmax_update(score_past(kj, g), value_block(kj, g), m_ref, l_ref, acc_ref, g,
                            slice(g * dv, (g + 1) * dv))
        return c

    lax.fori_loop(0, qi, body, 0)


def _causal_bias(t):
    kpos = lax.broadcasted_iota(jnp.int32, (t, t), 0)
    qpos = lax.broadcasted_iota(jnp.int32, (t, t), 1)
    return jnp.where(kpos <= qpos, 0.0, NEG_INF).astype(F32)


def _attn_finish(o_ref, l_ref, acc_ref, groups, dv):
    for g in range(groups):
        rows = slice(g * dv, (g + 1) * dv)
        acc_ref[rows, :] = acc_ref[rows, :] / l_ref[g]
    o_ref[...] = acc_ref[...].T.astype(o_ref.dtype)


def _moba_attn_kernel(q_ref, k_ref, vT_ref, km_ref, o_ref, sel_ref, qb_ref, s_ref, m_ref, l_ref, acc_ref):
    t = MOBA_BLOCK
    G = MOBA_GROUP
    dh = MOBA_HEAD_DIM
    qi = pl.program_id(2)
    nb = km_ref.shape[0]
    q_all = q_ref[...]
    lane = lax.broadcasted_iota(jnp.int32, q_all.shape, 1)
    blk = lax.broadcasted_iota(jnp.int32, (nb, t), 0)
    valid = blk < qi
    for g in range(G):
        qg = jnp.where(jnp.logical_and(lane >= g * dh, lane < (g + 1) * dh), q_all, 0.0)
        gate = _nt_dot(km_ref[...], qg, precision=lax.Precision.HIGHEST)
        gate = jnp.where(valid, gate, NEG_INF)
        rank = jnp.zeros((nb, t), F32)
        for n in range(nb):
            row = gate[n:n + 1, :]
            before = jnp.logical_or(row > gate, jnp.logical_and(row == gate, blk > n))
            rank = rank + jnp.where(before, 1.0, 0.0)
        chosen = jnp.logical_and(valid, rank < float(MOBA_TOPK))
        sel_ref[g] = jnp.where(chosen, 0.0, NEG_INF)
        qb_ref[g] = (qg * (dh ** -0.5 * LOG2_E)).astype(BF16)

    def score_own(g):
        q0 = pl.multiple_of(qi * t, t)
        return _nt_dot(k_ref[pl.ds(q0, t), :], qb_ref[g]) + _causal_bias(t)

    def score_past(kj, g):
        k0 = pl.multiple_of(kj * t, t)
        return _nt_dot(k_ref[pl.ds(k0, t), :], qb_ref[g]) + sel_ref[g, pl.ds(kj, 1), :]

    def value_block(blk_id, g):
        return vT_ref[0, g, :, pl.ds(pl.multiple_of(blk_id * t, t), t)]

    _attn_pipeline(qi, t, G, dh, score_own, score_past, value_block, s_ref, m_ref, l_ref, acc_ref)
    _attn_finish(o_ref, l_ref, acc_ref, G, dh)


def _moba_attn(q, k, vT, kmean, B, S):
    H, dh, G = MOBA_HEADS, MOBA_HEAD_DIM, MOBA_GROUP
    t = MOBA_BLOCK
    nb = S // t
    gw = G * dh
    return pl.pallas_call(
        _moba_attn_kernel,
        grid=(B, H // G, nb),
        in_specs=[pl.BlockSpec((t, gw), lambda b, h, i: (b * nb + i, h)),
                  pl.BlockSpec((S, gw), lambda b, h, i: (b, h)),
                  pl.BlockSpec((1, G, dh, S), lambda b, h, i: (b, h, 0, 0)),
                  pl.BlockSpec((nb, gw), lambda b, h, i: (b, h))],
        out_specs=pl.BlockSpec((t, gw), lambda b, h, i: (b * nb + i, h)),
        out_shape=jax.ShapeDtypeStruct((B * S, H * dh), BF16),
        scratch_shapes=[pltpu.VMEM((G, nb, t), F32), pltpu.VMEM((G, t, gw), BF16),
                        pltpu.VMEM((2, G, t, t), F32),
                        pltpu.VMEM((G, 1, t), F32), pltpu.VMEM((G, 1, t), F32),
                        pltpu.VMEM((gw, t), F32)],
        compiler_params=_cparams("parallel", "parallel", "arbitrary"),
        name="moba_attn",
    )(q, k, vT, kmean)


def _mla_attn_kernel(q_ref, k_ref, vT_ref, o_ref, m_ref, l_ref, acc_ref):
    t = MLA_TILE
    G = MLA_GROUP
    P = MLA_PAD
    dv = MLA_V
    qi = pl.program_id(2)

    def score(blk_id, g):
        cols = slice(g * P, (g + 1) * P)
        k0 = pl.multiple_of(blk_id * t, t)
        return _nt_dot(k_ref[pl.ds(k0, t), cols], q_ref[:, cols])

    def score_own(g):
        return score(qi, g) + _causal_bias(t)

    def value_block(blk_id, g):
        return vT_ref[0, g, :, pl.ds(pl.multiple_of(blk_id * t, t), t)]

    _attn_serial(qi, t, G, dv, score_own, score, value_block, m_ref, l_ref, acc_ref)
    _attn_finish(o_ref, l_ref, acc_ref, G, dv)


def _mla_attn(q, k, vT, B, S):
    H, G, P, dv = MLA_HEADS, MLA_GROUP, MLA_PAD, MLA_V
    t = MLA_TILE
    nq = S // t
    return pl.pallas_call(
        _mla_attn_kernel,
        grid=(B, H // G, nq),
        in_specs=[pl.BlockSpec((t, G * P), lambda b, h, i: (b * nq + i, h)),
                  pl.BlockSpec((S, G * P), lambda b, h, i: (b, h)),
                  pl.BlockSpec((1, G, dv, S), lambda b, h, i: (b, h, 0, 0))],
        out_specs=pl.BlockSpec((t, G * dv), lambda b, h, i: (b * nq + i, h)),
        out_shape=jax.ShapeDtypeStruct((B * S, H * dv), BF16),
        scratch_shapes=[pltpu.VMEM((G, 1, t), F32), pltpu.VMEM((G, 1, t), F32),
                        pltpu.VMEM((G * dv, t), F32)],
        compiler_params=_cparams("parallel", "parallel", "arbitrary"),
        name="mla_attn",
    )(q, k, vT)


def _merge_kernel(x_ref, ya_ref, yb_ref, wg_ref, bg_ref, wa_ref, wb_ref, wo_ref, g_ref, b_ref,
                  h_ref, hb_ref, *, alpha):
    x = x_ref[...]
    D = x.shape[1]
    xb = x.astype(BF16)
    ga = _sigmoid(_dot(xb, wg_ref[:, :D]) + bg_ref[:, :D])
    gb = _sigmoid(_dot(xb, wg_ref[:, D:]) + bg_ref[:, D:])
    merged = ga * _dot(ya_ref[...], wa_ref[...]) + gb * _dot(yb_ref[...], wb_ref[...])
    mix = _dot(merged.astype(BF16), wo_ref[...])
    h = _layer_norm(alpha * x + mix, g_ref[...], b_ref[...])
    h_ref[...] = h
    hb_ref[...] = h.astype(BF16)


def _merge(x2, ya, yb, wg, bg, wa, wb, wo, g, b, alpha):
    T, D = x2.shape
    tm = TOKEN_TILE
    row = lambda i: (i, 0)
    full = lambda a: pl.BlockSpec(a.shape, lambda i: (0, 0))
    return pl.pallas_call(
        functools.partial(_merge_kernel, alpha=alpha),
        grid=(T // tm,),
        in_specs=[pl.BlockSpec((tm, D), row), pl.BlockSpec((tm, ya.shape[1]), row),
                  pl.BlockSpec((tm, yb.shape[1]), row), full(wg), full(bg), full(wa), full(wb), full(wo),
                  full(g), full(b)],
        out_specs=[pl.BlockSpec((tm, D), row), pl.BlockSpec((tm, D), row)],
        out_shape=[jax.ShapeDtypeStruct((T, D), F32), jax.ShapeDtypeStruct((T, D), BF16)],
        compiler_params=_cparams("parallel"),
        name="merge_ln1",
    )(x2, ya, yb, wg, bg, wa, wb, wo, g, b)


def _peer_scores_kernel(h_ref, wq_ref, keys_ref, s_ref):
    q = _dot(h_ref[...], wq_ref[...]).astype(BF16)
    half = PEER_KEY_DIM // 2
    for g in range(2 * PEER_HEADS):
        s_ref[g] = _nt_dot(keys_ref[g], q[:, g * half:(g + 1) * half])


def _peer_scores(hb, wq, keys):
    T, D = hb.shape
    tm = TOKEN_TILE
    G = 2 * PEER_HEADS
    return pl.pallas_call(
        _peer_scores_kernel,
        grid=(T // tm,),
        in_specs=[pl.BlockSpec((tm, D), lambda i: (i, 0)),
                  pl.BlockSpec(wq.shape, lambda i: (0, 0)),
                  pl.BlockSpec(keys.shape, lambda i: (0, 0, 0))],
        out_specs=pl.BlockSpec((G, PEER_N_KEYS, tm), lambda i: (0, 0, i)),
        out_shape=jax.ShapeDtypeStruct((G, PEER_N_KEYS, T), F32),
        compiler_params=_cparams("parallel"),
        name="peer_scores",
    )(hb, wq, keys)


def _top_values(v, k):
    out = []
    for _ in range(k):
        m = jnp.max(v, axis=0, keepdims=True)
        out.append(m)
        v = jnp.where(v == m, -jnp.inf, v)
    return out


def _peer_select_kernel(s_ref, cnt_ref, ca_ref, rb_ref, eb_ref, cand_ref):
    tl = s_ref.shape[2]
    cand_ref[...] = jnp.full(cand_ref.shape, -jnp.inf, F32)
    for h in range(PEER_HEADS):
        a = s_ref[2 * h]
        b = s_ref[2 * h + 1]
        ta = _top_values(a, PEER_TOPK)
        tb = _top_values(b, PEER_TOPK)
        for c, (r, s) in enumerate(_PEER_CAND):
            cand_ref[pl.ds(c, 1), :] = ta[r] + tb[s]
        cand = cand_ref[...]
        tau = jnp.full((1, tl), jnp.inf, F32)
        for c in range(len(_PEER_CAND)):
            row = cand[c:c + 1, :]
            larger = jnp.sum(jnp.where(cand > row, 1.0, 0.0), axis=0, keepdims=True)
            tau = jnp.where(larger < float(PEER_TOPK), jnp.minimum(tau, row), tau)
        top = ta[0] + tb[0]
        z = jnp.sum(jnp.where(cand >= tau, jnp.exp(cand - top), 0.0), axis=0, keepdims=True)
        cnt = jnp.zeros(a.shape, F32)
        rb = jnp.zeros(b.shape, F32)
        for s in range(PEER_TOPK):
            cnt = cnt + jnp.where(a + tb[s] >= tau, 1.0, 0.0)
            rb = rb + jnp.where(tb[s] > b, 1.0, 0.0)
        cnt_ref[h] = cnt
        ca_ref[h] = jnp.exp(a - ta[0]) / z
        rb_ref[h] = rb.astype(BF16)
        eb_ref[h] = jnp.exp(b - tb[0]).astype(BF16)


def _peer_select(sT):
    G, N, T = sT.shape
    tl = PEER_SEL_TILE
    ospec = pl.BlockSpec((PEER_HEADS, N, tl), lambda i: (0, 0, i))
    shape = (PEER_HEADS, N, T)
    return pl.pallas_call(
        _peer_select_kernel,
        grid=(T // tl,),
        in_specs=[pl.BlockSpec((G, N, tl), lambda i: (0, 0, i))],
        out_specs=[ospec, ospec, ospec, ospec],
        out_shape=[jax.ShapeDtypeStruct(shape, F32), jax.ShapeDtypeStruct(shape, F32),
                   jax.ShapeDtypeStruct(shape, BF16), jax.ShapeDtypeStruct(shape, BF16)],
        scratch_shapes=[pltpu.VMEM((_PEER_CAND_ROWS, tl), F32)],
        compiler_params=_cparams("parallel"),
        name="peer_select",
    )(sT)


def _gelu(v):
    return 0.5 * v * (1.0 + lax.erf(v * (2.0 ** -0.5)))


def _peer_main_kernel(hT_ref, u_ref, vT_ref, cnt_ref, ca_ref, rb_ref, eb_ref, o_ref, wg_ref):
    e = pl.program_id(1)
    N = PEER_N_KEYS
    EH = PEER_EXPERT_HALF
    chains = PEER_EXPERT_TILE // EH
    per = EH // N
    tm = hT_ref.shape[1]

    @pl.when(e == 0)
    def _():
        o_ref[...] = jnp.zeros(o_ref.shape, o_ref.dtype)

    hT = hT_ref[...]
    contrib = None
    for c in range(chains):
        rows = slice(c * EH, (c + 1) * EH)
        act = _gelu(_dot(u_ref[rows, :], hT))
        for sub in range(per):
            i1 = (e * chains + c) * per + sub
            w = jnp.zeros((N, tm), BF16)
            for h in range(PEER_HEADS):
                cnt = cnt_ref[h, pl.ds(i1, 1), :].astype(BF16)
                ca = ca_ref[h, pl.ds(i1, 1), :].astype(BF16)
                w = w + jnp.where(rb_ref[h] < cnt, eb_ref[h], jnp.zeros((), BF16)) * ca
            r0 = c * EH + sub * N
            wg_ref[r0:r0 + N, :] = w * act[sub * N:(sub + 1) * N, :].astype(BF16)
        d = _dot(vT_ref[:, rows], wg_ref[rows, :])
        contrib = d if contrib is None else contrib + d
    o_ref[...] += contrib


def _peer_main(hT, u, vT, cnt, ca, rb, eb):
    D, T = hT.shape
    NE = u.shape[0]
    tm = PEER_TOKEN_TILE
    te = PEER_EXPERT_TILE
    stat = pl.BlockSpec((PEER_HEADS, PEER_N_KEYS, tm), lambda i, e: (0, 0, i))
    return pl.pallas_call(
        _peer_main_kernel,
        grid=(T // tm, NE // te),
        in_specs=[pl.BlockSpec((D, tm), lambda i, e: (0, i)),
                  pl.BlockSpec((te, D), lambda i, e: (e, 0)),
                  pl.BlockSpec((D, te), lambda i, e: (0, e)),
                  stat, stat, stat, stat],
        out_specs=pl.BlockSpec((D, tm), lambda i, e: (0, i)),
        out_shape=jax.ShapeDtypeStruct((D, T), F32),
        scratch_shapes=[pltpu.VMEM((te, tm), BF16)],
        compiler_params=_cparams("parallel", "arbitrary"),
        name="peer_main",
    )(hT, u, vT, cnt, ca, rb, eb)


def _final_kernel(h_ref, peer_ref, p_ref, wg_ref, bg_ref, wp_ref, g_ref, b_ref, o_ref, *, alpha):
    r = alpha * h_ref[...] + peer_ref[...]
    gate = _sigmoid(_dot(r.astype(BF16), wg_ref[...]) + bg_ref[...])
    ple = gate * _dot(p_ref[...].astype(BF16), wp_ref[...])
    o_ref[...] = _layer_norm(r + ple, g_ref[...], b_ref[...])


def _final(h, peer, p2, wg, bg, wp, g, b, alpha):
    T, D = h.shape
    tm = TOKEN_TILE
    row = lambda i: (i, 0)
    full = lambda a: pl.BlockSpec(a.shape, lambda i: (0, 0))
    return pl.pallas_call(
        functools.partial(_final_kernel, alpha=alpha),
        grid=(T // tm,),
        in_specs=[pl.BlockSpec((tm, D), row), pl.BlockSpec((tm, D), row),
                  pl.BlockSpec((tm, p2.shape[1]), row), full(wg), full(bg), full(wp), full(g), full(b)],
        out_specs=pl.BlockSpec((tm, D), row),
        out_shape=jax.ShapeDtypeStruct((T, D), F32),
        compiler_params=_cparams("parallel"),
        name="ple_ln2",
    )(h, peer, p2, wg, bg, wp, g, b)


def _rope_tables(seq, rot_dim, theta):
    inv = jnp.power(jnp.float32(theta), -jnp.arange(0, rot_dim, 2, dtype=F32) / rot_dim)
    ang = jnp.arange(seq).astype(F32)[:, None] * inv[None, :]
    ang = jnp.concatenate([ang, ang], axis=-1)
    return jnp.cos(ang), jnp.sin(ang)


def _rotate_half_cols(w):
    r = w.shape[-1]
    return jnp.concatenate([-w[..., r // 2:], w[..., :r // 2]], axis=-1)


def _pad_last(a, before, total):
    pad = [(0, 0)] * (a.ndim - 1) + [(before, total - before - a.shape[-1])]
    return jnp.pad(a, pad)


def _head_split_t(a, B, S, H, d):
    return a.reshape(B, S, H, d).transpose(0, 2, 3, 1)


def kernel(x, p, w_in, b_gates, mla_q_norm, w_q_up, mla_kv_norm, w_kv_up, w_branch_moba, w_branch_mla, w_out, ln1_g, ln1_b, peer_w_q, peer_sub_keys, peer_u, peer_v, w_ple_proj, w_ple_gate, b_ple_gate, ln2_g, ln2_b):
    B, S, D = x.shape
    depth = w_in.shape[0]
    T = B * S
    alpha = (2.0 * depth) ** 0.25
    H, dh = MOBA_HEADS, MOBA_HEAD_DIM

    cos_p, sin_p = _rope_tables(S, MOBA_ROT, MOBA_THETA)
    cos_a = jnp.tile(jnp.concatenate([cos_p, jnp.ones((S, dh - MOBA_ROT), F32)], axis=1), (1, LANES // dh))
    sin_a = jnp.tile(jnp.concatenate([sin_p, jnp.zeros((S, dh - MOBA_ROT), F32)], axis=1), (1, LANES // dh))
    cos_m, sin_m = _rope_tables(S, MLA_ROPE, MLA_THETA)
    cos_b = jnp.concatenate([jnp.ones((S, MLA_NOPE), F32), cos_m,
                             jnp.zeros((S, MLA_PAD - MLA_QK), F32)], axis=1)
    sin_b = _pad_last(sin_m, MLA_NOPE, MLA_PAD)

    o_k, o_v, o_cq = MOBA_W, 2 * MOBA_W, 3 * MOBA_W
    o_ckv = o_cq + MLA_Q_LORA
    o_kr = o_ckv + MLA_KV_LORA
    o_g = o_kr + MLA_ROPE

    def moba_rot(w):
        w3 = w.reshape(D, H, dh)
        return _pad_last(_rotate_half_cols(w3[..., :MOBA_ROT]), 0, dh).reshape(D, H * dh)

    x2 = x.reshape(T, D)
    for i in range(depth):
        w = w_in[i]
        wq, wk, wv = w[:, :o_k], w[:, o_k:o_v], w[:, o_v:o_cq]
        w_moba = jnp.concatenate([wq, moba_rot(wq), wk, moba_rot(wk), wv], axis=1).astype(BF16)
        q_a, k_a, v_a, kmean = _moba_proj(x2, w_moba, cos_a, sin_a, S)

        wkr = w[:, o_kr:o_g]
        wkr2 = jnp.concatenate([_pad_last(wkr, MLA_NOPE, MLA_PAD),
                                _pad_last(_rotate_half_cols(wkr), MLA_NOPE, MLA_PAD)], axis=1).astype(BF16)
        wqu = w_q_up[i].reshape(MLA_Q_LORA, MLA_HEADS, MLA_QK)
        wq_pad = _pad_last(wqu, 0, MLA_PAD).reshape(MLA_Q_LORA, -1).astype(BF16)
        wqr_pad = _pad_last(_rotate_half_cols(wqu[..., MLA_NOPE:]), MLA_NOPE, MLA_PAD)
        wqr_pad = wqr_pad.reshape(MLA_Q_LORA, -1).astype(BF16)
        wkv = w_kv_up[i].reshape(MLA_KV_LORA, MLA_HEADS, MLA_NOPE + MLA_V)
        wkn_pad = _pad_last(wkv[..., :MLA_NOPE], 0, MLA_PAD).reshape(MLA_KV_LORA, -1).astype(BF16)
        wvv = wkv[..., MLA_NOPE:].reshape(MLA_KV_LORA, -1).astype(BF16)
        q_m, k_m, v_m = _mla_proj(
            x2, w[:, o_cq:o_ckv].astype(BF16), w[:, o_ckv:o_kr].astype(BF16), wkr2,
            mla_q_norm[i][None, :], mla_kv_norm[i][None, :], wq_pad, wqr_pad, wkn_pad, wvv, cos_b, sin_b, S)

        y_a = _moba_attn(q_a, k_a, _head_split_t(v_a, B, S, H, dh), kmean.reshape(T // MOBA_BLOCK, MOBA_W), B, S)
        y_b = _mla_attn(q_m, k_m, _head_split_t(v_m, B, S, MLA_HEADS, MLA_V), B, S)

        h, hb = _merge(x2, y_a, y_b, w[:, o_g:].astype(BF16), b_gates[i].reshape(1, -1),
                       w_branch_moba[i].astype(BF16), w_branch_mla[i].astype(BF16), w_out[i].astype(BF16),
                       ln1_g[i][None, :], ln1_b[i][None, :], alpha)

        keys = peer_sub_keys[i].reshape(2 * PEER_HEADS, PEER_N_KEYS, PEER_KEY_DIM // 2).astype(BF16)
        sT = _peer_scores(hb, peer_w_q[i].astype(BF16), keys)
        cnt, ca, rb, eb = _peer_select(sT)
        peer_t = _peer_main(hb.T, peer_u[i].astype(BF16), peer_v[i].T.astype(BF16), cnt, ca, rb, eb)

        x2 = _final(h, peer_t.T, p[i].reshape(T, -1), w_ple_gate[i].astype(BF16), b_ple_gate[i][None, :],
                    w_ple_proj[i].astype(BF16), ln2_g[i][None, :], ln2_b[i][None, :], alpha)
    return x2.reshape(B, S, D)
```

```python
import functools

import jax
import jax.numpy as jnp
from jax import lax
from jax.experimental import pallas as pl
from jax.experimental.pallas import tpu as pltpu

F32 = jnp.float32
BF16 = jnp.bfloat16

LANES = 128
MOBA_HEADS = 8
MOBA_HEAD_DIM = 64
MOBA_BLOCK = 256
MOBA_TOPK = 3
MOBA_ROT = MOBA_HEAD_DIM // 4
MOBA_THETA = 500000.0
MOBA_W = MOBA_HEADS * MOBA_HEAD_DIM
MLA_HEADS = 8
MLA_Q_LORA = 512
MLA_KV_LORA = 256
MLA_NOPE = 64
MLA_ROPE = 32
MLA_V = 64
MLA_QK = MLA_NOPE + MLA_ROPE
MLA_PAD = LANES
MLA_THETA = 10000.0
PEER_HEADS = 8
PEER_N_KEYS = 128
PEER_KEY_DIM = 256
PEER_TOPK = 16
NEG_INF = -1e30
LOG2_E = 1.4426950408889634
LN_EPS = 1e-5
RMS_EPS = 1e-6

V7X_VMEM_LIMIT_BYTES = 48 * 1024 * 1024

TOKEN_TILE = 512
MOBA_GROUP = 4
MLA_TILE = 512
MLA_GROUP = 2
PEER_SEL_TILE = 256
PEER_TOKEN_TILE = 512
PEER_EXPERT_TILE = 2048
PEER_EXPERT_HALF = 256

_PEER_CAND = tuple((r, s) for r in range(PEER_TOPK) for s in range(PEER_TOPK)
                   if (r + 1) * (s + 1) <= PEER_TOPK)
_PEER_CAND_ROWS = -(-len(_PEER_CAND) // 8) * 8


def _cparams(*sem):
    return pltpu.CompilerParams(dimension_semantics=sem, vmem_limit_bytes=V7X_VMEM_LIMIT_BYTES)


def _nt_dot(a, b, precision=None):
    return lax.dot_general(a, b, (((1,), (1,)), ((), ())), preferred_element_type=F32,
                           precision=precision)


def _dot(a, b):
    return jnp.dot(a, b, preferred_element_type=F32)


def _layer_norm(v, g, b):
    mu = jnp.mean(v, axis=-1, keepdims=True)
    d = v - mu
    var = jnp.mean(d * d, axis=-1, keepdims=True)
    return d * lax.rsqrt(var + LN_EPS) * g + b


def _rms_norm(v, g):
    return v * lax.rsqrt(jnp.mean(v * v, axis=-1, keepdims=True) + RMS_EPS) * g


def _sigmoid(v):
    return 1.0 / (1.0 + jnp.exp(-v))


def _tile_lanes(v, n):
    return jnp.concatenate([v] * n, axis=1)


def _moba_proj_kernel(x_ref, w_ref, cos_ref, sin_ref, q_ref, k_ref, v_ref, km_ref):
    xb = x_ref[...].astype(BF16)
    W = MOBA_W
    cos = _tile_lanes(cos_ref[...], W // LANES)
    sin = _tile_lanes(sin_ref[...], W // LANES)
    q = _dot(xb, w_ref[:, 0:W]) * cos + _dot(xb, w_ref[:, W:2 * W]) * sin
    k = _dot(xb, w_ref[:, 2 * W:3 * W]) * cos + _dot(xb, w_ref[:, 3 * W:4 * W]) * sin
    v = _dot(xb, w_ref[:, 4 * W:5 * W])
    q_ref[...] = q
    k_ref[...] = k.astype(BF16)
    v_ref[0] = v.T.astype(BF16)
    nblk = k.shape[0] // MOBA_BLOCK
    km_ref[0] = jnp.sum(k.reshape(nblk, MOBA_BLOCK, W), axis=1) * (1.0 / MOBA_BLOCK)


def _moba_proj(x2, w, cos, sin, seq):
    T, D = x2.shape
    tm = TOKEN_TILE
    nblk = tm // MOBA_BLOCK
    ps = seq // tm
    row = lambda i: (i, 0)
    pos = lambda i: (i % ps, 0)
    return pl.pallas_call(
        _moba_proj_kernel,
        grid=(T // tm,),
        in_specs=[pl.BlockSpec((tm, D), row),
                  pl.BlockSpec(w.shape, lambda i: (0, 0)),
                  pl.BlockSpec((tm, LANES), pos),
                  pl.BlockSpec((tm, LANES), pos)],
        out_specs=[pl.BlockSpec((tm, MOBA_W), row),
                   pl.BlockSpec((tm, MOBA_W), row),
                   pl.BlockSpec((1, MOBA_W, tm), lambda i: (i // ps, 0, i % ps)),
                   pl.BlockSpec((1, nblk, MOBA_W), lambda i: (i, 0, 0))],
        out_shape=[jax.ShapeDtypeStruct((T, MOBA_W), F32),
                   jax.ShapeDtypeStruct((T, MOBA_W), BF16),
                   jax.ShapeDtypeStruct((T // seq, MOBA_W, seq), BF16),
                   jax.ShapeDtypeStruct((T // tm, nblk, MOBA_W), F32)],
        compiler_params=_cparams("parallel"),
        name="moba_proj",
    )(x2, w, cos, sin)


def _mla_proj_kernel(x_ref, wcq_ref, wckv_ref, wkr_ref, gq_ref, gkv_ref, wq_ref, wqr_ref,
                     wkn_ref, wv_ref, cos_ref, sin_ref, qm_ref, km_ref, v_ref):
    xb = x_ref[...].astype(BF16)
    cos1 = cos_ref[...]
    sin1 = sin_ref[...]
    cos = _tile_lanes(cos1, MLA_HEADS)
    sin = _tile_lanes(sin1, MLA_HEADS)
    scale = MLA_QK ** -0.5 * LOG2_E
    cq = _rms_norm(_dot(xb, wcq_ref[...]), gq_ref[...]).astype(BF16)
    q = _dot(cq, wq_ref[...]) * cos + _dot(cq, wqr_ref[...]) * sin
    qm_ref[...] = (q * scale).astype(BF16)
    ckv = _rms_norm(_dot(xb, wckv_ref[...]), gkv_ref[...]).astype(BF16)
    kr2 = _dot(xb, wkr_ref[...])
    kr = kr2[:, :MLA_PAD] * cos1 + kr2[:, MLA_PAD:] * sin1
    km_ref[...] = (_dot(ckv, wkn_ref[...]) + _tile_lanes(kr, MLA_HEADS)).astype(BF16)
    v_ref[0] = _dot(ckv, wv_ref[...]).T.astype(BF16)


def _mla_proj(x2, wcq, wckv, wkr2, gq, gkv, wq, wqr, wkn, wv, cos, sin, seq):
    T, D = x2.shape
    tm = TOKEN_TILE
    ps = seq // tm
    row = lambda i: (i, 0)
    pos = lambda i: (i % ps, 0)
    full = lambda a: pl.BlockSpec(a.shape, lambda i: (0, 0))
    HP = MLA_HEADS * MLA_PAD
    HV = MLA_HEADS * MLA_V
    return pl.pallas_call(
        _mla_proj_kernel,
        grid=(T // tm,),
        in_specs=[pl.BlockSpec((tm, D), row), full(wcq), full(wckv), full(wkr2), full(gq), full(gkv),
                  full(wq), full(wqr), full(wkn), full(wv),
                  pl.BlockSpec((tm, MLA_PAD), pos), pl.BlockSpec((tm, MLA_PAD), pos)],
        out_specs=[pl.BlockSpec((tm, HP), row), pl.BlockSpec((tm, HP), row),
                   pl.BlockSpec((1, HV, tm), lambda i: (i // ps, 0, i % ps))],
        out_shape=[jax.ShapeDtypeStruct((T, HP), BF16), jax.ShapeDtypeStruct((T, HP), BF16),
                   jax.ShapeDtypeStruct((T // seq, HV, seq), BF16)],
        compiler_params=_cparams("parallel"),
        name="mla_proj",
    )(x2, wcq, wckv, wkr2, gq, gkv, wq, wqr, wkn, wv, cos, sin)


def _softmax_update(sT, vT, m_ref, l_ref, acc_ref, g, rows):
    m_old = m_ref[g]
    m_new = jnp.maximum(m_old, jnp.max(sT, axis=0, keepdims=True))
    p = jnp.exp2(sT - m_new)
    alpha = jnp.exp2(m_old - m_new)
    m_ref[g] = m_new
    l_ref[g] = alpha * l_ref[g] + jnp.sum(p, axis=0, keepdims=True)
    acc_ref[rows, :] = alpha * acc_ref[rows, :] + _dot(vT, p.astype(BF16))


def _attn_pipeline(qi, t, groups, dv, score_own, score_past, value_block, m_ref, l_ref, acc_ref):
    m_ref[...] = jnp.full(m_ref.shape, NEG_INF, F32)
    l_ref[...] = jnp.zeros(l_ref.shape, F32)
    acc_ref[...] = jnp.zeros(acc_ref.shape, F32)

    def consume(scores, blk):
        for g in range(groups):
            _softmax_update(scores[g], value_block(blk, g), m_ref, l_ref, acc_ref, g,
                            slice(g * dv, (g + 1) * dv))

    def body(kj, scores):
        nxt = tuple(score_past(kj, g) for g in range(groups))
        consume(scores, jnp.where(kj == 0, qi, kj - 1))
        return nxt

    last = lax.fori_loop(0, qi, body, tuple(score_own(g) for g in range(groups)))
    consume(last, jnp.maximum(qi - 1, 0))


def _attn_serial(qi, t, groups, dv, score_own, score_past, value_block, m_ref, l_ref, acc_ref):
    m_ref[...] = jnp.full(m_ref.shape, NEG_INF, F32)
    l_ref[...] = jnp.zeros(l_ref.shape, F32)
    acc_ref[...] = jnp.zeros(acc_ref.shape, F32)
    for g in range(groups):
        _softmax_update(score_own(g), value_block(qi, g), m_ref, l_ref, acc_ref, g, slice(g * dv, (g + 1) * dv))

    def body(kj, c):
        for g in range(groups):
            _softmax_update(score_past(kj, g), value_block(kj, g), m_ref, l_ref, acc_ref, g,
                            slice(g * dv, (g + 1) * dv))
        return c

    lax.fori_loop(0, qi, body, 0)


def _causal_bias(t):
    kpos = lax.broadcasted_iota(jnp.int32, (t, t), 0)
    qpos = lax.broadcasted_iota(jnp.int32, (t, t), 1)
    return jnp.where(kpos <= qpos, 0.0, NEG_INF).astype(F32)


def _attn_finish(o_ref, l_ref, acc_ref, groups, dv):
    for g in range(groups):
        rows = slice(g * dv, (g + 1) * dv)
        acc_ref[rows, :] = acc_ref[rows, :] / l_ref[g]
    o_ref[...] = acc_ref[...].T.astype(o_ref.dtype)


def _moba_attn_kernel(q_ref, k_ref, vT_ref, km_ref, o_ref, sel_ref, qb_ref, m_ref, l_ref, acc_ref):
    t = MOBA_BLOCK
    G = MOBA_GROUP
    dh = MOBA_HEAD_DIM
    qi = pl.program_id(2)
    nb = km_ref.shape[0]
    q_all = q_ref[...]
    lane = lax.broadcasted_iota(jnp.int32, q_all.shape, 1)
    blk = lax.broadcasted_iota(jnp.int32, (nb, t), 0)
    valid = blk < qi
    for g in range(G):
        qg = jnp.where(jnp.logical_and(lane >= g * dh, lane < (g + 1) * dh), q_all, 0.0)
        gate = _nt_dot(km_ref[...], qg, precision=lax.Precision.HIGHEST)
        gate = jnp.where(valid, gate, NEG_INF)
        rank = jnp.zeros((nb, t), F32)
        for n in range(nb):
            row = gate[n:n + 1, :]
            before = jnp.logical_or(row > gate, jnp.logical_and(row == gate, blk > n))
            rank = rank + jnp.where(before, 1.0, 0.0)
        chosen = jnp.logical_and(valid, rank < float(MOBA_TOPK))
        sel_ref[g] = jnp.where(chosen, 0.0, NEG_INF)
        qb_ref[g] = (qg * (dh ** -0.5 * LOG2_E)).astype(BF16)

    def score_own(g):
        q0 = pl.multiple_of(qi * t, t)
        return _nt_dot(k_ref[pl.ds(q0, t), :], qb_ref[g]) + _causal_bias(t)

    def score_past(kj, g):
        k0 = pl.multiple_of(kj * t, t)
        return _nt_dot(k_ref[pl.ds(k0, t), :], qb_ref[g]) + sel_ref[g, pl.ds(kj, 1), :]

    def value_block(blk_id, g):
        return vT_ref[0, g, :, pl.ds(pl.multiple_of(blk_id * t, t), t)]

    _attn_pipeline(qi, t, G, dh, score_own, score_past, value_block, m_ref, l_ref, acc_ref)
    _attn_finish(o_ref, l_ref, acc_ref, G, dh)


def _moba_attn(q, k, vT, kmean, B, S):
    H, dh, G = MOBA_HEADS, MOBA_HEAD_DIM, MOBA_GROUP
    t = MOBA_BLOCK
    nb = S // t
    gw = G * dh
    return pl.pallas_call(
        _moba_attn_kernel,
        grid=(B, H // G, nb),
        in_specs=[pl.BlockSpec((t, gw), lambda b, h, i: (b * nb + i, h)),
                  pl.BlockSpec((S, gw), lambda b, h, i: (b, h)),
                  pl.BlockSpec((1, G, dh, S), lambda b, h, i: (b, h, 0, 0)),
                  pl.BlockSpec((nb, gw), lambda b, h, i: (b, h))],
        out_specs=pl.BlockSpec((t, gw), lambda b, h, i: (b * nb + i, h)),
        out_shape=jax.ShapeDtypeStruct((B * S, H * dh), BF16),
        scratch_shapes=[pltpu.VMEM((G, nb, t), F32), pltpu.VMEM((G, t, gw), BF16),
                        pltpu.VMEM((G, 1, t), F32), pltpu.VMEM((G, 1, t), F32),
                        pltpu.VMEM((gw, t), F32)],
        compiler_params=_cparams("parallel", "parallel", "arbitrary"),
        name="moba_attn",
    )(q, k, vT, kmean)


def _mla_attn_kernel(q_ref, k_ref, vT_ref, o_ref, m_ref, l_ref, acc_ref):
    t = MLA_TILE
    G = MLA_GROUP
    P = MLA_PAD
    dv = MLA_V
    qi = pl.program_id(2)

    def score(blk_id, g):
        cols = slice(g * P, (g + 1) * P)
        k0 = pl.multiple_of(blk_id * t, t)
        return _nt_dot(k_ref[pl.ds(k0, t), cols], q_ref[:, cols])

    def score_own(g):
        return score(qi, g) + _causal_bias(t)

    def value_block(blk_id, g):
        return vT_ref[0, g, :, pl.ds(pl.multiple_of(blk_id * t, t), t)]

    _attn_serial(qi, t, G, dv, score_own, score, value_block, m_ref, l_ref, acc_ref)
    _attn_finish(o_ref, l_ref, acc_ref, G, dv)


def _mla_attn(q, k, vT, B, S):
    H, G, P, dv = MLA_HEADS, MLA_GROUP, MLA_PAD, MLA_V
    t = MLA_TILE
    nq = S // t
    return pl.pallas_call(
        _mla_attn_kernel,
        grid=(B, H // G, nq),
        in_specs=[pl.BlockSpec((t, G * P), lambda b, h, i: (b * nq + i, h)),
                  pl.BlockSpec((S, G * P), lambda b, h, i: (b, h)),
                  pl.BlockSpec((1, G, dv, S), lambda b, h, i: (b, h, 0, 0))],
        out_specs=pl.BlockSpec((t, G * dv), lambda b, h, i: (b * nq + i, h)),
        out_shape=jax.ShapeDtypeStruct((B * S, H * dv), BF16),
        scratch_shapes=[pltpu.VMEM((G, 1, t), F32), pltpu.VMEM((G, 1, t), F32),
                        pltpu.VMEM((G * dv, t), F32)],
        compiler_params=_cparams("parallel", "parallel", "arbitrary"),
        name="mla_attn",
    )(q, k, vT)


def _merge_kernel(x_ref, ya_ref, yb_ref, wg_ref, bg_ref, wa_ref, wb_ref, wo_ref, g_ref, b_ref,
                  h_ref, hb_ref, hT_ref, *, alpha):
    x = x_ref[...]
    D = x.shape[1]
    xb = x.astype(BF16)
    ga = _sigmoid(_dot(xb, wg_ref[:, :D]) + bg_ref[:, :D])
    gb = _sigmoid(_dot(xb, wg_ref[:, D:]) + bg_ref[:, D:])
    merged = ga * _dot(ya_ref[...], wa_ref[...]) + gb * _dot(yb_ref[...], wb_ref[...])
    mix = _dot(merged.astype(BF16), wo_ref[...])
    h = _layer_norm(alpha * x + mix, g_ref[...], b_ref[...])
    h_ref[...] = h
    hb_ref[...] = h.astype(BF16)
    hT_ref[...] = h.T.astype(BF16)


def _merge(x2, ya, yb, wg, bg, wa, wb, wo, g, b, alpha):
    T, D = x2.shape
    tm = TOKEN_TILE
    row = lambda i: (i, 0)
    full = lambda a: pl.BlockSpec(a.shape, lambda i: (0, 0))
    return pl.pallas_call(
        functools.partial(_merge_kernel, alpha=alpha),
        grid=(T // tm,),
        in_specs=[pl.BlockSpec((tm, D), row), pl.BlockSpec((tm, ya.shape[1]), row),
                  pl.BlockSpec((tm, yb.shape[1]), row), full(wg), full(bg), full(wa), full(wb), full(wo),
                  full(g), full(b)],
        out_specs=[pl.BlockSpec((tm, D), row), pl.BlockSpec((tm, D), row),
                   pl.BlockSpec((D, tm), lambda i: (0, i))],
        out_shape=[jax.ShapeDtypeStruct((T, D), F32), jax.ShapeDtypeStruct((T, D), BF16),
                   jax.ShapeDtypeStruct((D, T), BF16)],
        compiler_params=_cparams("parallel"),
        name="merge_ln1",
    )(x2, ya, yb, wg, bg, wa, wb, wo, g, b)


def _peer_scores_kernel(h_ref, wq_ref, keys_ref, s_ref):
    q = _dot(h_ref[...], wq_ref[...]).astype(BF16)
    half = PEER_KEY_DIM // 2
    for g in range(2 * PEER_HEADS):
        s_ref[g] = _nt_dot(keys_ref[g], q[:, g * half:(g + 1) * half])


def _peer_scores(hb, wq, keys):
    T, D = hb.shape
    tm = TOKEN_TILE
    G = 2 * PEER_HEADS
    return pl.pallas_call(
        _peer_scores_kernel,
        grid=(T // tm,),
        in_specs=[pl.BlockSpec((tm, D), lambda i: (i, 0)),
                  pl.BlockSpec(wq.shape, lambda i: (0, 0)),
                  pl.BlockSpec(keys.shape, lambda i: (0, 0, 0))],
        out_specs=pl.BlockSpec((G, PEER_N_KEYS, tm), lambda i: (0, 0, i)),
        out_shape=jax.ShapeDtypeStruct((G, PEER_N_KEYS, T), F32),
        compiler_params=_cparams("parallel"),
        name="peer_scores",
    )(hb, wq, keys)


def _top_values(v, k):
    out = []
    for _ in range(k):
        m = jnp.max(v, axis=0, keepdims=True)
        out.append(m)
        v = jnp.where(v == m, -jnp.inf, v)
    return out


def _peer_select_kernel(s_ref, cnt_ref, ca_ref, rb_ref, eb_ref, cand_ref):
    tl = s_ref.shape[2]
    cand_ref[...] = jnp.full(cand_ref.shape, -jnp.inf, F32)
    for h in range(PEER_HEADS):
        a = s_ref[2 * h]
        b = s_ref[2 * h + 1]
        ta = _top_values(a, PEER_TOPK)
        tb = _top_values(b, PEER_TOPK)
        for c, (r, s) in enumerate(_PEER_CAND):
            cand_ref[pl.ds(c, 1), :] = ta[r] + tb[s]
        cand = cand_ref[...]
        tau = jnp.full((1, tl), jnp.inf, F32)
        for c in range(len(_PEER_CAND)):
            row = cand[c:c + 1, :]
            larger = jnp.sum(jnp.where(cand > row, 1.0, 0.0), axis=0, keepdims=True)
            tau = jnp.where(larger < float(PEER_TOPK), jnp.minimum(tau, row), tau)
        top = ta[0] + tb[0]
        z = jnp.sum(jnp.where(cand >= tau, jnp.exp(cand - top), 0.0), axis=0, keepdims=True)
        cnt = jnp.zeros(a.shape, F32)
        rb = jnp.zeros(b.shape, F32)
        for s in range(PEER_TOPK):
            cnt = cnt + jnp.where(a + tb[s] >= tau, 1.0, 0.0)
            rb = rb + jnp.where(tb[s] > b, 1.0, 0.0)
        cnt_ref[h] = cnt
        ca_ref[h] = jnp.exp(a - ta[0]) / z
        rb_ref[h] = rb.astype(BF16)
        eb_ref[h] = jnp.exp(b - tb[0]).astype(BF16)


def _peer_select(sT):
    G, N, T = sT.shape
    tl = PEER_SEL_TILE
    ospec = pl.BlockSpec((PEER_HEADS, N, tl), lambda i: (0, 0, i))
    shape = (PEER_HEADS, N, T)
    return pl.pallas_call(
        _peer_select_kernel,
        grid=(T // tl,),
        in_specs=[pl.BlockSpec((G, N, tl), lambda i: (0, 0, i))],
        out_specs=[ospec, ospec, ospec, ospec],
        out_shape=[jax.ShapeDtypeStruct(shape, F32), jax.ShapeDtypeStruct(shape, F32),
                   jax.ShapeDtypeStruct(shape, BF16), jax.ShapeDtypeStruct(shape, BF16)],
        scratch_shapes=[pltpu.VMEM((_PEER_CAND_ROWS, tl), F32)],
        compiler_params=_cparams("parallel"),
        name="peer_select",
    )(sT)


def _gelu(v):
    return 0.5 * v * (1.0 + lax.erf(v * (2.0 ** -0.5)))


def _peer_main_kernel(hT_ref, u_ref, vT_ref, cnt_ref, ca_ref, rb_ref, eb_ref, o_ref, wg_ref):
    e = pl.program_id(1)
    N = PEER_N_KEYS
    EH = PEER_EXPERT_HALF
    chains = PEER_EXPERT_TILE // EH
    per = EH // N
    tm = hT_ref.shape[1]

    @pl.when(e == 0)
    def _():
        o_ref[...] = jnp.zeros(o_ref.shape, o_ref.dtype)

    hT = hT_ref[...]
    contrib = None

    def value_matmul(c, contrib):
        rows = slice(c * EH, (c + 1) * EH)
        d = _dot(vT_ref[:, rows], wg_ref[rows, :])
        return d if contrib is None else contrib + d

    pre = _dot(u_ref[0:EH, :], hT)
    for c in range(chains):
        cur = pre
        if c + 1 < chains:
            pre = _dot(u_ref[(c + 1) * EH:(c + 2) * EH, :], hT)
        for sub in range(per):
            i1 = (e * chains + c) * per + sub
            w = jnp.zeros((N, tm), BF16)
            for h in range(PEER_HEADS):
                cnt = cnt_ref[h, pl.ds(i1, 1), :].astype(BF16)
                ca = ca_ref[h, pl.ds(i1, 1), :].astype(BF16)
                w = w + jnp.where(rb_ref[h] < cnt, eb_ref[h], jnp.zeros((), BF16)) * ca
            r0 = c * EH + sub * N
            wg_ref[r0:r0 + N, :] = w * _gelu(cur[sub * N:(sub + 1) * N, :]).astype(BF16)
            if sub == 0 and c > 0:
                contrib = value_matmul(c - 1, contrib)
    contrib = value_matmul(chains - 1, contrib)
    o_ref[...] += contrib


def _peer_main(hT, u, vT, cnt, ca, rb, eb):
    D, T = hT.shape
    NE = u.shape[0]
    tm = PEER_TOKEN_TILE
    te = PEER_EXPERT_TILE
    stat = pl.BlockSpec((PEER_HEADS, PEER_N_KEYS, tm), lambda i, e: (0, 0, i))
    return pl.pallas_call(
        _peer_main_kernel,
        grid=(T // tm, NE // te),
        in_specs=[pl.BlockSpec((D, tm), lambda i, e: (0, i)),
                  pl.BlockSpec((te, D), lambda i, e: (e, 0)),
                  pl.BlockSpec((D, te), lambda i, e: (0, e)),
                  stat, stat, stat, stat],
        out_specs=pl.BlockSpec((D, tm), lambda i, e: (0, i)),
        out_shape=jax.ShapeDtypeStruct((D, T), F32),
        scratch_shapes=[pltpu.VMEM((te, tm), BF16)],
        compiler_params=_cparams("parallel", "arbitrary"),
        name="peer_main",
    )(hT, u, vT, cnt, ca, rb, eb)


def _final_kernel(h_ref, peerT_ref, p_ref, wg_ref, bg_ref, wp_ref, g_ref, b_ref, o_ref, *, alpha):
    r = alpha * h_ref[...] + peerT_ref[...].T
    gate = _sigmoid(_dot(r.astype(BF16), wg_ref[...]) + bg_ref[...])
    ple = gate * _dot(p_ref[...].astype(BF16), wp_ref[...])
    o_ref[...] = _layer_norm(r + ple, g_ref[...], b_ref[...])


def _final(h, peer_t, p2, wg, bg, wp, g, b, alpha):
    T, D = h.shape
    tm = TOKEN_TILE
    row = lambda i: (i, 0)
    full = lambda a: pl.BlockSpec(a.shape, lambda i: (0, 0))
    return pl.pallas_call(
        functools.partial(_final_kernel, alpha=alpha),
        grid=(T // tm,),
        in_specs=[pl.BlockSpec((tm, D), row), pl.BlockSpec((D, tm), lambda i: (0, i)),
                  pl.BlockSpec((tm, p2.shape[1]), row), full(wg), full(bg), full(wp), full(g), full(b)],
        out_specs=pl.BlockSpec((tm, D), row),
        out_shape=jax.ShapeDtypeStruct((T, D), F32),
        compiler_params=_cparams("parallel"),
        name="ple_ln2",
    )(h, peer_t, p2, wg, bg, wp, g, b)


def _rope_tables(seq, rot_dim, theta):
    inv = jnp.power(jnp.float32(theta), -jnp.arange(0, rot_dim, 2, dtype=F32) / rot_dim)
    ang = jnp.arange(seq).astype(F32)[:, None] * inv[None, :]
    ang = jnp.concatenate([ang, ang], axis=-1)
    return jnp.cos(ang), jnp.sin(ang)


def _rotate_half_cols(w):
    r = w.shape[-1]
    return jnp.concatenate([-w[..., r // 2:], w[..., :r // 2]], axis=-1)


def _pad_last(a, before, total):
    pad = [(0, 0)] * (a.ndim - 1) + [(before, total - before - a.shape[-1])]
    return jnp.pad(a, pad)


def kernel(x, p, w_in, b_gates, mla_q_norm, w_q_up, mla_kv_norm, w_kv_up, w_branch_moba, w_branch_mla, w_out, ln1_g, ln1_b, peer_w_q, peer_sub_keys, peer_u, peer_v, w_ple_proj, w_ple_gate, b_ple_gate, ln2_g, ln2_b):
    B, S, D = x.shape
    depth = w_in.shape[0]
    T = B * S
    alpha = (2.0 * depth) ** 0.25
    H, dh = MOBA_HEADS, MOBA_HEAD_DIM

    cos_p, sin_p = _rope_tables(S, MOBA_ROT, MOBA_THETA)
    cos_a = jnp.tile(jnp.concatenate([cos_p, jnp.ones((S, dh - MOBA_ROT), F32)], axis=1), (1, LANES // dh))
    sin_a = jnp.tile(jnp.concatenate([sin_p, jnp.zeros((S, dh - MOBA_ROT), F32)], axis=1), (1, LANES // dh))
    cos_m, sin_m = _rope_tables(S, MLA_ROPE, MLA_THETA)
    cos_b = jnp.concatenate([jnp.ones((S, MLA_NOPE), F32), cos_m,
                             jnp.zeros((S, MLA_PAD - MLA_QK), F32)], axis=1)
    sin_b = _pad_last(sin_m, MLA_NOPE, MLA_PAD)

    o_k, o_v, o_cq = MOBA_W, 2 * MOBA_W, 3 * MOBA_W
    o_ckv = o_cq + MLA_Q_LORA
    o_kr = o_ckv + MLA_KV_LORA
    o_g = o_kr + MLA_ROPE

    def moba_rot(w):
        w3 = w.reshape(D, H, dh)
        return _pad_last(_rotate_half_cols(w3[..., :MOBA_ROT]), 0, dh).reshape(D, H * dh)

    x2 = x.reshape(T, D)
    for i in range(depth):
        w = w_in[i]
        wq, wk, wv = w[:, :o_k], w[:, o_k:o_v], w[:, o_v:o_cq]
        w_moba = jnp.concatenate([wq, moba_rot(wq), wk, moba_rot(wk), wv], axis=1).astype(BF16)
        q_a, k_a, v_a, kmean = _moba_proj(x2, w_moba, cos_a, sin_a, S)

        wkr = w[:, o_kr:o_g]
        wkr2 = jnp.concatenate([_pad_last(wkr, MLA_NOPE, MLA_PAD),
                                _pad_last(_rotate_half_cols(wkr), MLA_NOPE, MLA_PAD)], axis=1).astype(BF16)
        wqu = w_q_up[i].reshape(MLA_Q_LORA, MLA_HEADS, MLA_QK)
        wq_pad = _pad_last(wqu, 0, MLA_PAD).reshape(MLA_Q_LORA, -1).astype(BF16)
        wqr_pad = _pad_last(_rotate_half_cols(wqu[..., MLA_NOPE:]), MLA_NOPE, MLA_PAD)
        wqr_pad = wqr_pad.reshape(MLA_Q_LORA, -1).astype(BF16)
        wkv = w_kv_up[i].reshape(MLA_KV_LORA, MLA_HEADS, MLA_NOPE + MLA_V)
        wkn_pad = _pad_last(wkv[..., :MLA_NOPE], 0, MLA_PAD).reshape(MLA_KV_LORA, -1).astype(BF16)
        wvv = wkv[..., MLA_NOPE:].reshape(MLA_KV_LORA, -1).astype(BF16)
        q_m, k_m, v_m = _mla_proj(
            x2, w[:, o_cq:o_ckv].astype(BF16), w[:, o_ckv:o_kr].astype(BF16), wkr2,
            mla_q_norm[i][None, :], mla_kv_norm[i][None, :], wq_pad, wqr_pad, wkn_pad, wvv, cos_b, sin_b, S)

        y_a = _moba_attn(q_a, k_a, v_a.reshape(B, H, dh, S), kmean.reshape(T // MOBA_BLOCK, MOBA_W), B, S)
        y_b = _mla_attn(q_m, k_m, v_m.reshape(B, MLA_HEADS, MLA_V, S), B, S)

        h, hb, hT = _merge(x2, y_a, y_b, w[:, o_g:].astype(BF16), b_gates[i].reshape(1, -1),
                       w_branch_moba[i].astype(BF16), w_branch_mla[i].astype(BF16), w_out[i].astype(BF16),
                       ln1_g[i][None, :], ln1_b[i][None, :], alpha)

        keys = peer_sub_keys[i].reshape(2 * PEER_HEADS, PEER_N_KEYS, PEER_KEY_DIM // 2).astype(BF16)
        sT = _peer_scores(hb, peer_w_q[i].astype(BF16), keys)
        cnt, ca, rb, eb = _peer_select(sT)
        peer_t = _peer_main(hT, peer_u[i].astype(BF16), peer_v[i].T.astype(BF16), cnt, ca, rb, eb)

        x2 = _final(h, peer_t, p[i].reshape(T, -1), w_ple_gate[i].astype(BF16), b_ple_gate[i][None, :],
                    w_ple_proj[i].astype(BF16), ln2_g[i][None, :], ln2_b[i][None, :], alpha)
    return x2.reshape(B, S, D)
```

```python
import functools

import jax
import jax.numpy as jnp
from jax import lax
from jax.experimental import pallas as pl
from jax.experimental.pallas import tpu as pltpu

F32 = jnp.float32
BF16 = jnp.bfloat16

LANES = 128
MOBA_HEADS = 8
MOBA_HEAD_DIM = 64
MOBA_BLOCK = 256
MOBA_TOPK = 3
MOBA_ROT = MOBA_HEAD_DIM // 4
MOBA_THETA = 500000.0
MOBA_W = MOBA_HEADS * MOBA_HEAD_DIM
MLA_HEADS = 8
MLA_Q_LORA = 512
MLA_KV_LORA = 256
MLA_NOPE = 64
MLA_ROPE = 32
MLA_V = 64
MLA_QK = MLA_NOPE + MLA_ROPE
MLA_PAD = LANES
MLA_THETA = 10000.0
PEER_HEADS = 8
PEER_N_KEYS = 128
PEER_KEY_DIM = 256
PEER_TOPK = 16
NEG_INF = -1e30
LOG2_E = 1.4426950408889634
LN_EPS = 1e-5
RMS_EPS = 1e-6

V7X_VMEM_LIMIT_BYTES = 48 * 1024 * 1024

TOKEN_TILE = 512
MOBA_GROUP = 4
MLA_TILE = 512
MLA_GROUP = 2
PEER_SEL_TILE = 256
PEER_TOKEN_TILE = 512
PEER_EXPERT_TILE = 2048
PEER_EXPERT_HALF = 512

_PEER_CAND = tuple((r, s) for r in range(PEER_TOPK) for s in range(PEER_TOPK)
                   if (r + 1) * (s + 1) <= PEER_TOPK)
_PEER_CAND_ROWS = -(-len(_PEER_CAND) // 8) * 8


def _cparams(*sem):
    return pltpu.CompilerParams(dimension_semantics=sem, vmem_limit_bytes=V7X_VMEM_LIMIT_BYTES)


def _nt_dot(a, b, precision=None):
    return lax.dot_general(a, b, (((1,), (1,)), ((), ())), preferred_element_type=F32,
                           precision=precision)


def _dot(a, b):
    return jnp.dot(a, b, preferred_element_type=F32)


def _layer_norm(v, g, b):
    mu = jnp.mean(v, axis=-1, keepdims=True)
    d = v - mu
    var = jnp.mean(d * d, axis=-1, keepdims=True)
    return d * lax.rsqrt(var + LN_EPS) * g + b


def _rms_norm(v, g):
    return v * lax.rsqrt(jnp.mean(v * v, axis=-1, keepdims=True) + RMS_EPS) * g


def _sigmoid(v):
    return 1.0 / (1.0 + jnp.exp(-v))


def _tile_lanes(v, n):
    return jnp.concatenate([v] * n, axis=1)


def _moba_proj_kernel(x_ref, w_ref, cos_ref, sin_ref, q_ref, k_ref, v_ref, km_ref):
    xb = x_ref[...].astype(BF16)
    W = MOBA_W
    cos = _tile_lanes(cos_ref[...], W // LANES)
    sin = _tile_lanes(sin_ref[...], W // LANES)
    q = _dot(xb, w_ref[:, 0:W]) * cos + _dot(xb, w_ref[:, W:2 * W]) * sin
    k = _dot(xb, w_ref[:, 2 * W:3 * W]) * cos + _dot(xb, w_ref[:, 3 * W:4 * W]) * sin
    v = _dot(xb, w_ref[:, 4 * W:5 * W])
    q_ref[...] = q
    k_ref[...] = k.astype(BF16)
    v_ref[0] = v.T.astype(BF16)
    nblk = k.shape[0] // MOBA_BLOCK
    km_ref[0] = jnp.sum(k.reshape(nblk, MOBA_BLOCK, W), axis=1) * (1.0 / MOBA_BLOCK)


def _moba_proj(x2, w, cos, sin, seq):
    T, D = x2.shape
    tm = TOKEN_TILE
    nblk = tm // MOBA_BLOCK
    ps = seq // tm
    row = lambda i: (i, 0)
    pos = lambda i: (i % ps, 0)
    return pl.pallas_call(
        _moba_proj_kernel,
        grid=(T // tm,),
        in_specs=[pl.BlockSpec((tm, D), row),
                  pl.BlockSpec(w.shape, lambda i: (0, 0)),
                  pl.BlockSpec((tm, LANES), pos),
                  pl.BlockSpec((tm, LANES), pos)],
        out_specs=[pl.BlockSpec((tm, MOBA_W), row),
                   pl.BlockSpec((tm, MOBA_W), row),
                   pl.BlockSpec((1, MOBA_W, tm), lambda i: (i // ps, 0, i % ps)),
                   pl.BlockSpec((1, nblk, MOBA_W), lambda i: (i, 0, 0))],
        out_shape=[jax.ShapeDtypeStruct((T, MOBA_W), F32),
                   jax.ShapeDtypeStruct((T, MOBA_W), BF16),
                   jax.ShapeDtypeStruct((T // seq, MOBA_W, seq), BF16),
                   jax.ShapeDtypeStruct((T // tm, nblk, MOBA_W), F32)],
        compiler_params=_cparams("parallel"),
        name="moba_proj",
    )(x2, w, cos, sin)


def _mla_proj_kernel(x_ref, wcq_ref, wckv_ref, wkr_ref, gq_ref, gkv_ref, wq_ref, wqr_ref,
                     wkn_ref, wv_ref, cos_ref, sin_ref, qm_ref, km_ref, v_ref):
    xb = x_ref[...].astype(BF16)
    cos1 = cos_ref[...]
    sin1 = sin_ref[...]
    cos = _tile_lanes(cos1, MLA_HEADS)
    sin = _tile_lanes(sin1, MLA_HEADS)
    scale = MLA_QK ** -0.5 * LOG2_E
    cq = _rms_norm(_dot(xb, wcq_ref[...]), gq_ref[...]).astype(BF16)
    q = _dot(cq, wq_ref[...]) * cos + _dot(cq, wqr_ref[...]) * sin
    qm_ref[...] = (q * scale).astype(BF16)
    ckv = _rms_norm(_dot(xb, wckv_ref[...]), gkv_ref[...]).astype(BF16)
    kr2 = _dot(xb, wkr_ref[...])
    kr = kr2[:, :MLA_PAD] * cos1 + kr2[:, MLA_PAD:] * sin1
    km_ref[...] = (_dot(ckv, wkn_ref[...]) + _tile_lanes(kr, MLA_HEADS)).astype(BF16)
    v_ref[0] = _dot(ckv, wv_ref[...]).T.astype(BF16)


def _mla_proj(x2, wcq, wckv, wkr2, gq, gkv, wq, wqr, wkn, wv, cos, sin, seq):
    T, D = x2.shape
    tm = TOKEN_TILE
    ps = seq // tm
    row = lambda i: (i, 0)
    pos = lambda i: (i % ps, 0)
    full = lambda a: pl.BlockSpec(a.shape, lambda i: (0, 0))
    HP = MLA_HEADS * MLA_PAD
    HV = MLA_HEADS * MLA_V
    return pl.pallas_call(
        _mla_proj_kernel,
        grid=(T // tm,),
        in_specs=[pl.BlockSpec((tm, D), row), full(wcq), full(wckv), full(wkr2), full(gq), full(gkv),
                  full(wq), full(wqr), full(wkn), full(wv),
                  pl.BlockSpec((tm, MLA_PAD), pos), pl.BlockSpec((tm, MLA_PAD), pos)],
        out_specs=[pl.BlockSpec((tm, HP), row), pl.BlockSpec((tm, HP), row),
                   pl.BlockSpec((1, HV, tm), lambda i: (i // ps, 0, i % ps))],
        out_shape=[jax.ShapeDtypeStruct((T, HP), BF16), jax.ShapeDtypeStruct((T, HP), BF16),
                   jax.ShapeDtypeStruct((T // seq, HV, seq), BF16)],
        compiler_params=_cparams("parallel"),
        name="mla_proj",
    )(x2, wcq, wckv, wkr2, gq, gkv, wq, wqr, wkn, wv, cos, sin)


def _softmax_update(sT, vT, m_ref, l_ref, acc_ref, g, rows):
    m_old = m_ref[g]
    m_new = jnp.maximum(m_old, jnp.max(sT, axis=0, keepdims=True))
    p = jnp.exp2(sT - m_new)
    alpha = jnp.exp2(m_old - m_new)
    m_ref[g] = m_new
    l_ref[g] = alpha * l_ref[g] + jnp.sum(p, axis=0, keepdims=True)
    acc_ref[rows, :] = alpha * acc_ref[rows, :] + _dot(vT, p.astype(BF16))


def _attn_pipeline(qi, t, groups, dv, score_own, score_past, value_block, m_ref, l_ref, acc_ref):
    m_ref[...] = jnp.full(m_ref.shape, NEG_INF, F32)
    l_ref[...] = jnp.zeros(l_ref.shape, F32)
    acc_ref[...] = jnp.zeros(acc_ref.shape, F32)

    def consume(scores, blk):
        for g in range(groups):
            _softmax_update(scores[g], value_block(blk, g), m_ref, l_ref, acc_ref, g,
                            slice(g * dv, (g + 1) * dv))

    def body(kj, scores):
        nxt = tuple(score_past(kj, g) for g in range(groups))
        consume(scores, jnp.where(kj == 0, qi, kj - 1))
        return nxt

    last = lax.fori_loop(0, qi, body, tuple(score_own(g) for g in range(groups)))
    consume(last, jnp.maximum(qi - 1, 0))


def _attn_serial(qi, t, groups, dv, score_own, score_past, value_block, m_ref, l_ref, acc_ref):
    m_ref[...] = jnp.full(m_ref.shape, NEG_INF, F32)
    l_ref[...] = jnp.zeros(l_ref.shape, F32)
    acc_ref[...] = jnp.zeros(acc_ref.shape, F32)
    for g in range(groups):
        _softmax_update(score_own(g), value_block(qi, g), m_ref, l_ref, acc_ref, g, slice(g * dv, (g + 1) * dv))

    def body(kj, c):
        for g in range(groups):
            _softmax_update(score_past(kj, g), value_block(kj, g), m_ref, l_ref, acc_ref, g,
                            slice(g * dv, (g + 1) * dv))
        return c

    lax.fori_loop(0, qi, body, 0)


def _causal_bias(t):
    kpos = lax.broadcasted_iota(jnp.int32, (t, t), 0)
    qpos = lax.broadcasted_iota(jnp.int32, (t, t), 1)
    return jnp.where(kpos <= qpos, 0.0, NEG_INF).astype(F32)


def _attn_finish(o_ref, l_ref, acc_ref, groups, dv):
    for g in range(groups):
        rows = slice(g * dv, (g + 1) * dv)
        acc_ref[rows, :] = acc_ref[rows, :] / l_ref[g]
    o_ref[...] = acc_ref[...].T.astype(o_ref.dtype)


def _moba_attn_kernel(q_ref, k_ref, vT_ref, km_ref, o_ref, sel_ref, qb_ref, m_ref, l_ref, acc_ref):
    t = MOBA_BLOCK
    G = MOBA_GROUP
    dh = MOBA_HEAD_DIM
    qi = pl.program_id(2)
    nb = km_ref.shape[0]
    q_all = q_ref[...]
    lane = lax.broadcasted_iota(jnp.int32, q_all.shape, 1)
    blk = lax.broadcasted_iota(jnp.int32, (nb, t), 0)
    valid = blk < qi
    for g in range(G):
        qg = jnp.where(jnp.logical_and(lane >= g * dh, lane < (g + 1) * dh), q_all, 0.0)
        gate = _nt_dot(km_ref[...], qg, precision=lax.Precision.HIGHEST)
        gate = jnp.where(valid, gate, NEG_INF)
        rank = jnp.zeros((nb, t), F32)
        for n in range(nb):
            row = gate[n:n + 1, :]
            before = jnp.logical_or(row > gate, jnp.logical_and(row == gate, blk > n))
            rank = rank + jnp.where(before, 1.0, 0.0)
        chosen = jnp.logical_and(valid, rank < float(MOBA_TOPK))
        sel_ref[g] = jnp.where(chosen, 0.0, NEG_INF)
        qb_ref[g] = (qg * (dh ** -0.5 * LOG2_E)).astype(BF16)

    def score_own(g):
        q0 = pl.multiple_of(qi * t, t)
        return _nt_dot(k_ref[pl.ds(q0, t), :], qb_ref[g]) + _causal_bias(t)

    def score_past(kj, g):
        k0 = pl.multiple_of(kj * t, t)
        return _nt_dot(k_ref[pl.ds(k0, t), :], qb_ref[g]) + sel_ref[g, pl.ds(kj, 1), :]

    def value_block(blk_id, g):
        return vT_ref[0, g, :, pl.ds(pl.multiple_of(blk_id * t, t), t)]

    _attn_pipeline(qi, t, G, dh, score_own, score_past, value_block, m_ref, l_ref, acc_ref)
    _attn_finish(o_ref, l_ref, acc_ref, G, dh)


def _moba_attn(q, k, vT, kmean, B, S):
    H, dh, G = MOBA_HEADS, MOBA_HEAD_DIM, MOBA_GROUP
    t = MOBA_BLOCK
    nb = S // t
    gw = G * dh
    return pl.pallas_call(
        _moba_attn_kernel,
        grid=(B, H // G, nb),
        in_specs=[pl.BlockSpec((t, gw), lambda b, h, i: (b * nb + i, h)),
                  pl.BlockSpec((S, gw), lambda b, h, i: (b, h)),
                  pl.BlockSpec((1, G, dh, S), lambda b, h, i: (b, h, 0, 0)),
                  pl.BlockSpec((nb, gw), lambda b, h, i: (b, h))],
        out_specs=pl.BlockSpec((t, gw), lambda b, h, i: (b * nb + i, h)),
        out_shape=jax.ShapeDtypeStruct((B * S, H * dh), BF16),
        scratch_shapes=[pltpu.VMEM((G, nb, t), F32), pltpu.VMEM((G, t, gw), BF16),
                        pltpu.VMEM((G, 1, t), F32), pltpu.VMEM((G, 1, t), F32),
                        pltpu.VMEM((gw, t), F32)],
        compiler_params=_cparams("parallel", "parallel", "arbitrary"),
        name="moba_attn",
    )(q, k, vT, kmean)


def _mla_attn_kernel(q_ref, k_ref, vT_ref, o_ref, m_ref, l_ref, acc_ref):
    t = MLA_TILE
    G = MLA_GROUP
    P = MLA_PAD
    dv = MLA_V
    qi = pl.program_id(2)

    def score(blk_id, g):
        cols = slice(g * P, (g + 1) * P)
        k0 = pl.multiple_of(blk_id * t, t)
        return _nt_dot(k_ref[pl.ds(k0, t), cols], q_ref[:, cols])

    def score_own(g):
        return score(qi, g) + _causal_bias(t)

    def value_block(blk_id, g):
        return vT_ref[0, g, :, pl.ds(pl.multiple_of(blk_id * t, t), t)]

    _attn_serial(qi, t, G, dv, score_own, score, value_block, m_ref, l_ref, acc_ref)
    _attn_finish(o_ref, l_ref, acc_ref, G, dv)


def _mla_attn(q, k, vT, B, S):
    H, G, P, dv = MLA_HEADS, MLA_GROUP, MLA_PAD, MLA_V
    t = MLA_TILE
    nq = S // t
    return pl.pallas_call(
        _mla_attn_kernel,
        grid=(B, H // G, nq),
        in_specs=[pl.BlockSpec((t, G * P), lambda b, h, i: (b * nq + i, h)),
                  pl.BlockSpec((S, G * P), lambda b, h, i: (b, h)),
                  pl.BlockSpec((1, G, dv, S), lambda b, h, i: (b, h, 0, 0))],
        out_specs=pl.BlockSpec((t, G * dv), lambda b, h, i: (b * nq + i, h)),
        out_shape=jax.ShapeDtypeStruct((B * S, H * dv), BF16),
        scratch_shapes=[pltpu.VMEM((G, 1, t), F32), pltpu.VMEM((G, 1, t), F32),
                        pltpu.VMEM((G * dv, t), F32)],
        compiler_params=_cparams("parallel", "parallel", "arbitrary"),
        name="mla_attn",
    )(q, k, vT)


def _merge_kernel(x_ref, ya_ref, yb_ref, wg_ref, bg_ref, wa_ref, wb_ref, wo_ref, g_ref, b_ref,
                  h_ref, hb_ref, hT_ref, *, alpha):
    x = x_ref[...]
    D = x.shape[1]
    xb = x.astype(BF16)
    ga = _sigmoid(_dot(xb, wg_ref[:, :D]) + bg_ref[:, :D])
    gb = _sigmoid(_dot(xb, wg_ref[:, D:]) + bg_ref[:, D:])
    merged = ga * _dot(ya_ref[...], wa_ref[...]) + gb * _dot(yb_ref[...], wb_ref[...])
    mix = _dot(merged.astype(BF16), wo_ref[...])
    h = _layer_norm(alpha * x + mix, g_ref[...], b_ref[...])
    h_ref[...] = h
    hb_ref[...] = h.astype(BF16)
    hT_ref[...] = h.T.astype(BF16)


def _merge(x2, ya, yb, wg, bg, wa, wb, wo, g, b, alpha):
    T, D = x2.shape
    tm = TOKEN_TILE
    row = lambda i: (i, 0)
    full = lambda a: pl.BlockSpec(a.shape, lambda i: (0, 0))
    return pl.pallas_call(
        functools.partial(_merge_kernel, alpha=alpha),
        grid=(T // tm,),
        in_specs=[pl.BlockSpec((tm, D), row), pl.BlockSpec((tm, ya.shape[1]), row),
                  pl.BlockSpec((tm, yb.shape[1]), row), full(wg), full(bg), full(wa), full(wb), full(wo),
                  full(g), full(b)],
        out_specs=[pl.BlockSpec((tm, D), row), pl.BlockSpec((tm, D), row),
                   pl.BlockSpec((D, tm), lambda i: (0, i))],
        out_shape=[jax.ShapeDtypeStruct((T, D), F32), jax.ShapeDtypeStruct((T, D), BF16),
                   jax.ShapeDtypeStruct((D, T), BF16)],
        compiler_params=_cparams("parallel"),
        name="merge_ln1",
    )(x2, ya, yb, wg, bg, wa, wb, wo, g, b)


def _peer_scores_kernel(h_ref, wq_ref, keys_ref, s_ref):
    q = _dot(h_ref[...], wq_ref[...]).astype(BF16)
    half = PEER_KEY_DIM // 2
    for g in range(2 * PEER_HEADS):
        s_ref[g] = _nt_dot(keys_ref[g], q[:, g * half:(g + 1) * half])


def _peer_scores(hb, wq, keys):
    T, D = hb.shape
    tm = TOKEN_TILE
    G = 2 * PEER_HEADS
    return pl.pallas_call(
        _peer_scores_kernel,
        grid=(T // tm,),
        in_specs=[pl.BlockSpec((tm, D), lambda i: (i, 0)),
                  pl.BlockSpec(wq.shape, lambda i: (0, 0)),
                  pl.BlockSpec(keys.shape, lambda i: (0, 0, 0))],
        out_specs=pl.BlockSpec((G, PEER_N_KEYS, tm), lambda i: (0, 0, i)),
        out_shape=jax.ShapeDtypeStruct((G, PEER_N_KEYS, T), F32),
        compiler_params=_cparams("parallel"),
        name="peer_scores",
    )(hb, wq, keys)


def _top_values(v, k, top_ref, slot):
    rank = jnp.full(v.shape, float(k), F32)
    for r in range(k):
        m = jnp.max(v, axis=0, keepdims=True)
        top_ref[slot, pl.ds(r, 1), :] = m
        hit = v == m
        rank = jnp.where(hit, float(r), rank)
        v = jnp.where(hit, -jnp.inf, v)
    return rank


def _peer_select_kernel(s_ref, cnt_ref, ca_ref, rb_ref, eb_ref, cand_ref, top_ref, ra_ref, cbr_ref):
    tl = s_ref.shape[2]
    cand_ref[...] = jnp.full(cand_ref.shape, -jnp.inf, F32)
    row = lambda slot, r: top_ref[slot, pl.ds(r, 1), :]
    for h in range(PEER_HEADS):
        rb_ref[h] = _top_values(s_ref[2 * h + 1], PEER_TOPK, top_ref, 1).astype(BF16)
        eb_ref[h] = jnp.exp(s_ref[2 * h + 1] - row(1, 0)).astype(BF16)
        ra_ref[...] = _top_values(s_ref[2 * h], PEER_TOPK, top_ref, 0)
        for c, (r, s) in enumerate(_PEER_CAND):
            cand_ref[pl.ds(c, 1), :] = row(0, r) + row(1, s)
        cand = cand_ref[...]
        rest = cand
        for _ in range(PEER_TOPK):
            tau = jnp.max(rest, axis=0, keepdims=True)
            rest = jnp.where(rest == tau, -jnp.inf, rest)
        z = jnp.sum(jnp.where(cand >= tau, jnp.exp(cand - cand[0:1, :]), 0.0), axis=0, keepdims=True)
        c = 0
        for r in range(PEER_TOPK):
            n = jnp.zeros((1, tl), F32)
            while c < len(_PEER_CAND) and _PEER_CAND[c][0] == r:
                n = n + jnp.where(cand[c:c + 1, :] >= tau, 1.0, 0.0)
                c += 1
            cbr_ref[pl.ds(r, 1), :] = n
        ra = ra_ref[...]
        cnt = jnp.zeros(ra.shape, F32)
        for r in range(PEER_TOPK):
            cnt = jnp.where(ra == float(r), cbr_ref[pl.ds(r, 1), :], cnt)
        cnt_ref[h] = cnt
        ca_ref[h] = jnp.exp(s_ref[2 * h] - row(0, 0)) / z


def _peer_select(sT):
    G, N, T = sT.shape
    tl = PEER_SEL_TILE
    ospec = pl.BlockSpec((PEER_HEADS, N, tl), lambda i: (0, 0, i))
    shape = (PEER_HEADS, N, T)
    return pl.pallas_call(
        _peer_select_kernel,
        grid=(T // tl,),
        in_specs=[pl.BlockSpec((G, N, tl), lambda i: (0, 0, i))],
        out_specs=[ospec, ospec, ospec, ospec],
        out_shape=[jax.ShapeDtypeStruct(shape, F32), jax.ShapeDtypeStruct(shape, F32),
                   jax.ShapeDtypeStruct(shape, BF16), jax.ShapeDtypeStruct(shape, BF16)],
        scratch_shapes=[pltpu.VMEM((_PEER_CAND_ROWS, tl), F32), pltpu.VMEM((2, PEER_TOPK, tl), F32),
                        pltpu.VMEM((N, tl), F32), pltpu.VMEM((PEER_TOPK, tl), F32)],
        compiler_params=_cparams("parallel"),
        name="peer_select",
    )(sT)


def _gelu(v):
    return 0.5 * v * (1.0 + lax.erf(v * (2.0 ** -0.5)))


def _peer_main_kernel(hT_ref, u_ref, vT_ref, cnt_ref, ca_ref, rb_ref, eb_ref, o_ref, wg_ref):
    e = pl.program_id(1)
    N = PEER_N_KEYS
    EH = PEER_EXPERT_HALF
    chains = PEER_EXPERT_TILE // EH
    per = EH // N
    tm = hT_ref.shape[1]

    @pl.when(e == 0)
    def _():
        o_ref[...] = jnp.zeros(o_ref.shape, o_ref.dtype)

    hT = hT_ref[...]
    contrib = None

    def value_matmul(c, contrib):
        rows = slice(c * EH, (c + 1) * EH)
        d = _dot(vT_ref[:, rows], wg_ref[rows, :])
        return d if contrib is None else contrib + d

    pre = _dot(u_ref[0:EH, :], hT)
    for c in range(chains):
        cur = pre
        if c + 1 < chains:
            pre = _dot(u_ref[(c + 1) * EH:(c + 2) * EH, :], hT)
        for sub in range(per):
            i1 = (e * chains + c) * per + sub
            w = jnp.zeros((N, tm), BF16)
            for h in range(PEER_HEADS):
                cnt = cnt_ref[h, pl.ds(i1, 1), :].astype(BF16)
                ca = ca_ref[h, pl.ds(i1, 1), :].astype(BF16)
                w = w + jnp.where(rb_ref[h] < cnt, eb_ref[h], jnp.zeros((), BF16)) * ca
            r0 = c * EH + sub * N
            wg_ref[r0:r0 + N, :] = w * _gelu(cur[sub * N:(sub + 1) * N, :]).astype(BF16)
            if sub == 0 and c > 0:
                contrib = value_matmul(c - 1, contrib)
    contrib = value_matmul(chains - 1, contrib)
    o_ref[...] += contrib


def _peer_main(hT, u, vT, cnt, ca, rb, eb):
    D, T = hT.shape
    NE = u.shape[0]
    tm = PEER_TOKEN_TILE
    te = PEER_EXPERT_TILE
    stat = pl.BlockSpec((PEER_HEADS, PEER_N_KEYS, tm), lambda i, e: (0, 0, i))
    return pl.pallas_call(
        _peer_main_kernel,
        grid=(T // tm, NE // te),
        in_specs=[pl.BlockSpec((D, tm), lambda i, e: (0, i)),
                  pl.BlockSpec((te, D), lambda i, e: (e, 0)),
                  pl.BlockSpec((D, te), lambda i, e: (0, e)),
                  stat, stat, stat, stat],
        out_specs=pl.BlockSpec((D, tm), lambda i, e: (0, i)),
        out_shape=jax.ShapeDtypeStruct((D, T), F32),
        scratch_shapes=[pltpu.VMEM((te, tm), BF16)],
        compiler_params=_cparams("parallel", "arbitrary"),
        name="peer_main",
    )(hT, u, vT, cnt, ca, rb, eb)


def _final_kernel(h_ref, peerT_ref, p_ref, wg_ref, bg_ref, wp_ref, g_ref, b_ref, o_ref, *, alpha):
    r = alpha * h_ref[...] + peerT_ref[...].T
    gate = _sigmoid(_dot(r.astype(BF16), wg_ref[...]) + bg_ref[...])
    ple = gate * _dot(p_ref[...].astype(BF16), wp_ref[...])
    o_ref[...] = _layer_norm(r + ple, g_ref[...], b_ref[...])


def _final(h, peer_t, p2, wg, bg, wp, g, b, alpha):
    T, D = h.shape
    tm = TOKEN_TILE
    row = lambda i: (i, 0)
    full = lambda a: pl.BlockSpec(a.shape, lambda i: (0, 0))
    return pl.pallas_call(
        functools.partial(_final_kernel, alpha=alpha),
        grid=(T // tm,),
        in_specs=[pl.BlockSpec((tm, D), row), pl.BlockSpec((D, tm), lambda i: (0, i)),
                  pl.BlockSpec((tm, p2.shape[1]), row), full(wg), full(bg), full(wp), full(g), full(b)],
        out_specs=pl.BlockSpec((tm, D), row),
        out_shape=jax.ShapeDtypeStruct((T, D), F32),
        compiler_params=_cparams("parallel"),
        name="ple_ln2",
    )(h, peer_t, p2, wg, bg, wp, g, b)


def _rope_tables(seq, rot_dim, theta):
    inv = jnp.power(jnp.float32(theta), -jnp.arange(0, rot_dim, 2, dtype=F32) / rot_dim)
    ang = jnp.arange(seq).astype(F32)[:, None] * inv[None, :]
    ang = jnp.concatenate([ang, ang], axis=-1)
    return jnp.cos(ang), jnp.sin(ang)


def _rotate_half_cols(w):
    r = w.shape[-1]
    return jnp.concatenate([-w[..., r // 2:], w[..., :r // 2]], axis=-1)


def _pad_last(a, before, total):
    pad = [(0, 0)] * (a.ndim - 1) + [(before, total - before - a.shape[-1])]
    return jnp.pad(a, pad)


def kernel(x, p, w_in, b_gates, mla_q_norm, w_q_up, mla_kv_norm, w_kv_up, w_branch_moba, w_branch_mla, w_out, ln1_g, ln1_b, peer_w_q, peer_sub_keys, peer_u, peer_v, w_ple_proj, w_ple_gate, b_ple_gate, ln2_g, ln2_b):
    B, S, D = x.shape
    depth = w_in.shape[0]
    T = B * S
    alpha = (2.0 * depth) ** 0.25
    H, dh = MOBA_HEADS, MOBA_HEAD_DIM

    cos_p, sin_p = _rope_tables(S, MOBA_ROT, MOBA_THETA)
    cos_a = jnp.tile(jnp.concatenate([cos_p, jnp.ones((S, dh - MOBA_ROT), F32)], axis=1), (1, LANES // dh))
    sin_a = jnp.tile(jnp.concatenate([sin_p, jnp.zeros((S, dh - MOBA_ROT), F32)], axis=1), (1, LANES // dh))
    cos_m, sin_m = _rope_tables(S, MLA_ROPE, MLA_THETA)
    cos_b = jnp.concatenate([jnp.ones((S, MLA_NOPE), F32), cos_m,
                             jnp.zeros((S, MLA_PAD - MLA_QK), F32)], axis=1)
    sin_b = _pad_last(sin_m, MLA_NOPE, MLA_PAD)

    o_k, o_v, o_cq = MOBA_W, 2 * MOBA_W, 3 * MOBA_W
    o_ckv = o_cq + MLA_Q_LORA
    o_kr = o_ckv + MLA_KV_LORA
    o_g = o_kr + MLA_ROPE

    def moba_rot(w):
        w3 = w.reshape(D, H, dh)
        return _pad_last(_rotate_half_cols(w3[..., :MOBA_ROT]), 0, dh).reshape(D, H * dh)

    x2 = x.reshape(T, D)
    for i in range(depth):
        w = w_in[i]
        wq, wk, wv = w[:, :o_k], w[:, o_k:o_v], w[:, o_v:o_cq]
        w_moba = jnp.concatenate([wq, moba_rot(wq), wk, moba_rot(wk), wv], axis=1).astype(BF16)
        q_a, k_a, v_a, kmean = _moba_proj(x2, w_moba, cos_a, sin_a, S)

        wkr = w[:, o_kr:o_g]
        wkr2 = jnp.concatenate([_pad_last(wkr, MLA_NOPE, MLA_PAD),
                                _pad_last(_rotate_half_cols(wkr), MLA_NOPE, MLA_PAD)], axis=1).astype(BF16)
        wqu = w_q_up[i].reshape(MLA_Q_LORA, MLA_HEADS, MLA_QK)
        wq_pad = _pad_last(wqu, 0, MLA_PAD).reshape(MLA_Q_LORA, -1).astype(BF16)
        wqr_pad = _pad_last(_rotate_half_cols(wqu[..., MLA_NOPE:]), MLA_NOPE, MLA_PAD)
        wqr_pad = wqr_pad.reshape(MLA_Q_LORA, -1).astype(BF16)
        wkv = w_kv_up[i].reshape(MLA_KV_LORA, MLA_HEADS, MLA_NOPE + MLA_V)
        wkn_pad = _pad_last(wkv[..., :MLA_NOPE], 0, MLA_PAD).reshape(MLA_KV_LORA, -1).astype(BF16)
        wvv = wkv[..., MLA_NOPE:].reshape(MLA_KV_LORA, -1).astype(BF16)
        q_m, k_m, v_m = _mla_proj(
            x2, w[:, o_cq:o_ckv].astype(BF16), w[:, o_ckv:o_kr].astype(BF16), wkr2,
            mla_q_norm[i][None, :], mla_kv_norm[i][None, :], wq_pad, wqr_pad, wkn_pad, wvv, cos_b, sin_b, S)

        y_a = _moba_attn(q_a, k_a, v_a.reshape(B, H, dh, S), kmean.reshape(T // MOBA_BLOCK, MOBA_W), B, S)
        y_b = _mla_attn(q_m, k_m, v_m.reshape(B, MLA_HEADS, MLA_V, S), B, S)

        h, hb, hT = _merge(x2, y_a, y_b, w[:, o_g:].astype(BF16), b_gates[i].reshape(1, -1),
                       w_branch_moba[i].astype(BF16), w_branch_mla[i].astype(BF16), w_out[i].astype(BF16),
                       ln1_g[i][None, :], ln1_b[i][None, :], alpha)

        keys = peer_sub_keys[i].reshape(2 * PEER_HEADS, PEER_N_KEYS, PEER_KEY_DIM // 2).astype(BF16)
        sT = _peer_scores(hb, peer_w_q[i].astype(BF16), keys)
        cnt, ca, rb, eb = _peer_select(sT)
        peer_t = _peer_main(hT, peer_u[i].astype(BF16), peer_v[i].T.astype(BF16), cnt, ca, rb, eb)

        x2 = _final(h, peer_t, p[i].reshape(T, -1), w_ple_gate[i].astype(BF16), b_ple_gate[i][None, :],
                    w_ple_proj[i].astype(BF16), ln2_g[i][None, :], ln2_b[i][None, :], alpha)
    return x2.reshape(B, S, D)
```

```python
import functools

import jax
import jax.numpy as jnp
from jax import lax
from jax.experimental import pallas as pl
from jax.experimental.pallas import tpu as pltpu

F32 = jnp.float32
BF16 = jnp.bfloat16

LANES = 128
MOBA_HEADS = 8
MOBA_HEAD_DIM = 64
MOBA_BLOCK = 256
MOBA_TOPK = 3
MOBA_ROT = MOBA_HEAD_DIM // 4
MOBA_THETA = 500000.0
MOBA_W = MOBA_HEADS * MOBA_HEAD_DIM
MLA_HEADS = 8
MLA_Q_LORA = 512
MLA_KV_LORA = 256
MLA_NOPE = 64
MLA_ROPE = 32
MLA_V = 64
MLA_QK = MLA_NOPE + MLA_ROPE
MLA_PAD = LANES
MLA_THETA = 10000.0
PEER_HEADS = 8
PEER_N_KEYS = 128
PEER_KEY_DIM = 256
PEER_TOPK = 16
NEG_INF = -1e30
LOG2_E = 1.4426950408889634
LN_EPS = 1e-5
RMS_EPS = 1e-6

V7X_VMEM_LIMIT_BYTES = 48 * 1024 * 1024

TOKEN_TILE = 512
MOBA_GROUP = 4
MLA_TILE = 512
MLA_GROUP = 4
PEER_SEL_TILE = 256
PEER_TOKEN_TILE = 512
PEER_EXPERT_TILE = 2048
PEER_EXPERT_HALF = 512

_PEER_CAND = tuple((r, s) for r in range(PEER_TOPK) for s in range(PEER_TOPK)
                   if (r + 1) * (s + 1) <= PEER_TOPK)
_PEER_CAND_ROWS = -(-len(_PEER_CAND) // 8) * 8


def _cparams(*sem):
    return pltpu.CompilerParams(dimension_semantics=sem, vmem_limit_bytes=V7X_VMEM_LIMIT_BYTES)


def _nt_dot(a, b, precision=None):
    return lax.dot_general(a, b, (((1,), (1,)), ((), ())), preferred_element_type=F32,
                           precision=precision)


def _dot(a, b):
    return jnp.dot(a, b, preferred_element_type=F32)


def _layer_norm(v, g, b):
    mu = jnp.mean(v, axis=-1, keepdims=True)
    d = v - mu
    var = jnp.mean(d * d, axis=-1, keepdims=True)
    return d * lax.rsqrt(var + LN_EPS) * g + b


def _rms_norm(v, g):
    return v * lax.rsqrt(jnp.mean(v * v, axis=-1, keepdims=True) + RMS_EPS) * g


def _sigmoid(v):
    return 1.0 / (1.0 + jnp.exp(-v))


def _tile_lanes(v, n):
    return jnp.concatenate([v] * n, axis=1)


def _moba_proj_kernel(x_ref, w_ref, cos_ref, sin_ref, q_ref, k_ref, v_ref, km_ref):
    xb = x_ref[...].astype(BF16)
    W = MOBA_W
    cos = _tile_lanes(cos_ref[...], W // LANES)
    sin = _tile_lanes(sin_ref[...], W // LANES)
    q = _dot(xb, w_ref[:, 0:W]) * cos + _dot(xb, w_ref[:, W:2 * W]) * sin
    k = _dot(xb, w_ref[:, 2 * W:3 * W]) * cos + _dot(xb, w_ref[:, 3 * W:4 * W]) * sin
    v = _dot(xb, w_ref[:, 4 * W:5 * W])
    q_ref[...] = q
    k_ref[...] = k.astype(BF16)
    v_ref[0] = v.T.astype(BF16)
    nblk = k.shape[0] // MOBA_BLOCK
    km_ref[0] = jnp.sum(k.reshape(nblk, MOBA_BLOCK, W), axis=1) * (1.0 / MOBA_BLOCK)


def _moba_proj(x2, w, cos, sin, seq):
    T, D = x2.shape
    tm = TOKEN_TILE
    nblk = tm // MOBA_BLOCK
    ps = seq // tm
    row = lambda i: (i, 0)
    pos = lambda i: (i % ps, 0)
    return pl.pallas_call(
        _moba_proj_kernel,
        grid=(T // tm,),
        in_specs=[pl.BlockSpec((tm, D), row),
                  pl.BlockSpec(w.shape, lambda i: (0, 0)),
                  pl.BlockSpec((tm, LANES), pos),
                  pl.BlockSpec((tm, LANES), pos)],
        out_specs=[pl.BlockSpec((tm, MOBA_W), row),
                   pl.BlockSpec((tm, MOBA_W), row),
                   pl.BlockSpec((1, MOBA_W, tm), lambda i: (i // ps, 0, i % ps)),
                   pl.BlockSpec((1, nblk, MOBA_W), lambda i: (i, 0, 0))],
        out_shape=[jax.ShapeDtypeStruct((T, MOBA_W), F32),
                   jax.ShapeDtypeStruct((T, MOBA_W), BF16),
                   jax.ShapeDtypeStruct((T // seq, MOBA_W, seq), BF16),
                   jax.ShapeDtypeStruct((T // tm, nblk, MOBA_W), F32)],
        compiler_params=_cparams("parallel"),
        name="moba_proj",
    )(x2, w, cos, sin)


def _mla_proj_kernel(x_ref, wcq_ref, wckv_ref, wkr_ref, gq_ref, gkv_ref, wq_ref, wqr_ref,
                     wkn_ref, wv_ref, cos_ref, sin_ref, qm_ref, km_ref, v_ref):
    xb = x_ref[...].astype(BF16)
    cos1 = cos_ref[...]
    sin1 = sin_ref[...]
    cos = _tile_lanes(cos1, MLA_HEADS)
    sin = _tile_lanes(sin1, MLA_HEADS)
    scale = MLA_QK ** -0.5 * LOG2_E
    cq = _rms_norm(_dot(xb, wcq_ref[...]), gq_ref[...]).astype(BF16)
    q = _dot(cq, wq_ref[...]) * cos + _dot(cq, wqr_ref[...]) * sin
    qm_ref[...] = (q * scale).astype(BF16)
    ckv = _rms_norm(_dot(xb, wckv_ref[...]), gkv_ref[...]).astype(BF16)
    kr2 = _dot(xb, wkr_ref[...])
    kr = kr2[:, :MLA_PAD] * cos1 + kr2[:, MLA_PAD:] * sin1
    km_ref[...] = (_dot(ckv, wkn_ref[...]) + _tile_lanes(kr, MLA_HEADS)).astype(BF16)
    v_ref[0] = _dot(ckv, wv_ref[...]).T.astype(BF16)


def _mla_proj(x2, wcq, wckv, wkr2, gq, gkv, wq, wqr, wkn, wv, cos, sin, seq):
    T, D = x2.shape
    tm = TOKEN_TILE
    ps = seq // tm
    row = lambda i: (i, 0)
    pos = lambda i: (i % ps, 0)
    full = lambda a: pl.BlockSpec(a.shape, lambda i: (0, 0))
    HP = MLA_HEADS * MLA_PAD
    HV = MLA_HEADS * MLA_V
    return pl.pallas_call(
        _mla_proj_kernel,
        grid=(T // tm,),
        in_specs=[pl.BlockSpec((tm, D), row), full(wcq), full(wckv), full(wkr2), full(gq), full(gkv),
                  full(wq), full(wqr), full(wkn), full(wv),
                  pl.BlockSpec((tm, MLA_PAD), pos), pl.BlockSpec((tm, MLA_PAD), pos)],
        out_specs=[pl.BlockSpec((tm, HP), row), pl.BlockSpec((tm, HP), row),
                   pl.BlockSpec((1, HV, tm), lambda i: (i // ps, 0, i % ps))],
        out_shape=[jax.ShapeDtypeStruct((T, HP), BF16), jax.ShapeDtypeStruct((T, HP), BF16),
                   jax.ShapeDtypeStruct((T // seq, HV, seq), BF16)],
        compiler_params=_cparams("parallel"),
        name="mla_proj",
    )(x2, wcq, wckv, wkr2, gq, gkv, wq, wqr, wkn, wv, cos, sin)


def _softmax_update(sT, vT, m_ref, l_ref, acc_ref, g, rows):
    m_old = m_ref[g]
    m_new = jnp.maximum(m_old, jnp.max(sT, axis=0, keepdims=True))
    p = jnp.exp2(sT - m_new)
    alpha = jnp.exp2(m_old - m_new)
    m_ref[g] = m_new
    l_ref[g] = alpha * l_ref[g] + jnp.sum(p, axis=0, keepdims=True)
    acc_ref[rows, :] = alpha * acc_ref[rows, :] + _dot(vT, p.astype(BF16))


def _attn_pipeline(qi, t, groups, dv, score_own, score_past, value_block, m_ref, l_ref, acc_ref):
    m_ref[...] = jnp.full(m_ref.shape, NEG_INF, F32)
    l_ref[...] = jnp.zeros(l_ref.shape, F32)
    acc_ref[...] = jnp.zeros(acc_ref.shape, F32)

    def consume(scores, blk):
        for g in range(groups):
            _softmax_update(scores[g], value_block(blk, g), m_ref, l_ref, acc_ref, g,
                            slice(g * dv, (g + 1) * dv))

    def body(kj, scores):
        nxt = tuple(score_past(kj, g) for g in range(groups))
        consume(scores, jnp.where(kj == 0, qi, kj - 1))
        return nxt

    last = lax.fori_loop(0, qi, body, tuple(score_own(g) for g in range(groups)))
    consume(last, jnp.maximum(qi - 1, 0))


def _attn_serial(qi, t, groups, dv, score_own, score_past, value_block, m_ref, l_ref, acc_ref):
    m_ref[...] = jnp.full(m_ref.shape, NEG_INF, F32)
    l_ref[...] = jnp.zeros(l_ref.shape, F32)
    acc_ref[...] = jnp.zeros(acc_ref.shape, F32)
    for g in range(groups):
        _softmax_update(score_own(g), value_block(qi, g), m_ref, l_ref, acc_ref, g, slice(g * dv, (g + 1) * dv))

    def body(kj, c):
        for g in range(groups):
            _softmax_update(score_past(kj, g), value_block(kj, g), m_ref, l_ref, acc_ref, g,
                            slice(g * dv, (g + 1) * dv))
        return c

    lax.fori_loop(0, qi, body, 0)


def _causal_bias(t):
    kpos = lax.broadcasted_iota(jnp.int32, (t, t), 0)
    qpos = lax.broadcasted_iota(jnp.int32, (t, t), 1)
    return jnp.where(kpos <= qpos, 0.0, NEG_INF).astype(F32)


def _attn_finish(o_ref, l_ref, acc_ref, groups, dv):
    for g in range(groups):
        rows = slice(g * dv, (g + 1) * dv)
        acc_ref[rows, :] = acc_ref[rows, :] / l_ref[g]
    o_ref[...] = acc_ref[...].T.astype(o_ref.dtype)


def _moba_attn_kernel(q_ref, k_ref, vT_ref, km_ref, o_ref, sel_ref, qb_ref, m_ref, l_ref, acc_ref):
    t = MOBA_BLOCK
    G = MOBA_GROUP
    dh = MOBA_HEAD_DIM
    qi = pl.program_id(2)
    nb = km_ref.shape[0]
    q_all = q_ref[...]
    lane = lax.broadcasted_iota(jnp.int32, q_all.shape, 1)
    blk = lax.broadcasted_iota(jnp.int32, (nb, t), 0)
    valid = blk < qi
    for g in range(G):
        qg = jnp.where(jnp.logical_and(lane >= g * dh, lane < (g + 1) * dh), q_all, 0.0)
        gate = _nt_dot(km_ref[...], qg, precision=lax.Precision.HIGHEST)
        gate = jnp.where(valid, gate, NEG_INF)
        rank = jnp.zeros((nb, t), F32)
        for n in range(nb):
            row = gate[n:n + 1, :]
            before = jnp.logical_or(row > gate, jnp.logical_and(row == gate, blk > n))
            rank = rank + jnp.where(before, 1.0, 0.0)
        chosen = jnp.logical_and(valid, rank < float(MOBA_TOPK))
        sel_ref[g] = jnp.where(chosen, 0.0, NEG_INF)
        qb_ref[g] = (qg * (dh ** -0.5 * LOG2_E)).astype(BF16)

    def score_own(g):
        q0 = pl.multiple_of(qi * t, t)
        return _nt_dot(k_ref[pl.ds(q0, t), :], qb_ref[g]) + _causal_bias(t)

    def score_past(kj, g):
        k0 = pl.multiple_of(kj * t, t)
        return _nt_dot(k_ref[pl.ds(k0, t), :], qb_ref[g]) + sel_ref[g, pl.ds(kj, 1), :]

    def value_block(blk_id, g):
        return vT_ref[0, g, :, pl.ds(pl.multiple_of(blk_id * t, t), t)]

    _attn_pipeline(qi, t, G, dh, score_own, score_past, value_block, m_ref, l_ref, acc_ref)
    _attn_finish(o_ref, l_ref, acc_ref, G, dh)


def _moba_attn(q, k, vT, kmean, B, S):
    H, dh, G = MOBA_HEADS, MOBA_HEAD_DIM, MOBA_GROUP
    t = MOBA_BLOCK
    nb = S // t
    gw = G * dh
    return pl.pallas_call(
        _moba_attn_kernel,
        grid=(B, H // G, nb),
        in_specs=[pl.BlockSpec((t, gw), lambda b, h, i: (b * nb + i, h)),
                  pl.BlockSpec((S, gw), lambda b, h, i: (b, h)),
                  pl.BlockSpec((1, G, dh, S), lambda b, h, i: (b, h, 0, 0)),
                  pl.BlockSpec((nb, gw), lambda b, h, i: (b, h))],
        out_specs=pl.BlockSpec((t, gw), lambda b, h, i: (b * nb + i, h)),
        out_shape=jax.ShapeDtypeStruct((B * S, H * dh), BF16),
        scratch_shapes=[pltpu.VMEM((G, nb, t), F32), pltpu.VMEM((G, t, gw), BF16),
                        pltpu.VMEM((G, 1, t), F32), pltpu.VMEM((G, 1, t), F32),
                        pltpu.VMEM((gw, t), F32)],
        compiler_params=_cparams("parallel", "parallel", "arbitrary"),
        name="moba_attn",
    )(q, k, vT, kmean)


def _mla_attn_kernel(q_ref, k_ref, vT_ref, o_ref, m_ref, l_ref, acc_ref):
    t = MLA_TILE
    G = MLA_GROUP
    P = MLA_PAD
    dv = MLA_V
    qi = pl.program_id(2)

    def score(blk_id, g):
        cols = slice(g * P, (g + 1) * P)
        k0 = pl.multiple_of(blk_id * t, t)
        return _nt_dot(k_ref[pl.ds(k0, t), cols], q_ref[:, cols])

    def score_own(g):
        return score(qi, g) + _causal_bias(t)

    def value_block(blk_id, g):
        return vT_ref[0, g, :, pl.ds(pl.multiple_of(blk_id * t, t), t)]

    _attn_serial(qi, t, G, dv, score_own, score, value_block, m_ref, l_ref, acc_ref)
    _attn_finish(o_ref, l_ref, acc_ref, G, dv)


def _mla_attn(q, k, vT, B, S):
    H, G, P, dv = MLA_HEADS, MLA_GROUP, MLA_PAD, MLA_V
    t = MLA_TILE
    nq = S // t
    return pl.pallas_call(
        _mla_attn_kernel,
        grid=(B, H // G, nq),
        in_specs=[pl.BlockSpec((t, G * P), lambda b, h, i: (b * nq + i, h)),
                  pl.BlockSpec((S, G * P), lambda b, h, i: (b, h)),
                  pl.BlockSpec((1, G, dv, S), lambda b, h, i: (b, h, 0, 0))],
        out_specs=pl.BlockSpec((t, G * dv), lambda b, h, i: (b * nq + i, h)),
        out_shape=jax.ShapeDtypeStruct((B * S, H * dv), BF16),
        scratch_shapes=[pltpu.VMEM((G, 1, t), F32), pltpu.VMEM((G, 1, t), F32),
                        pltpu.VMEM((G * dv, t), F32)],
        compiler_params=_cparams("parallel", "parallel", "arbitrary"),
        name="mla_attn",
    )(q, k, vT)


def _merge_kernel(x_ref, ya_ref, yb_ref, wg_ref, bg_ref, wa_ref, wb_ref, wo_ref, g_ref, b_ref,
                  h_ref, hb_ref, hT_ref, *, alpha):
    x = x_ref[...]
    D = x.shape[1]
    xb = x.astype(BF16)
    ga = _sigmoid(_dot(xb, wg_ref[:, :D]) + bg_ref[:, :D])
    gb = _sigmoid(_dot(xb, wg_ref[:, D:]) + bg_ref[:, D:])
    merged = ga * _dot(ya_ref[...], wa_ref[...]) + gb * _dot(yb_ref[...], wb_ref[...])
    mix = _dot(merged.astype(BF16), wo_ref[...])
    h = _layer_norm(alpha * x + mix, g_ref[...], b_ref[...])
    h_ref[...] = h
    hb_ref[...] = h.astype(BF16)
    hT_ref[...] = h.T.astype(BF16)


def _merge(x2, ya, yb, wg, bg, wa, wb, wo, g, b, alpha):
    T, D = x2.shape
    tm = TOKEN_TILE
    row = lambda i: (i, 0)
    full = lambda a: pl.BlockSpec(a.shape, lambda i: (0, 0))
    return pl.pallas_call(
        functools.partial(_merge_kernel, alpha=alpha),
        grid=(T // tm,),
        in_specs=[pl.BlockSpec((tm, D), row), pl.BlockSpec((tm, ya.shape[1]), row),
                  pl.BlockSpec((tm, yb.shape[1]), row), full(wg), full(bg), full(wa), full(wb), full(wo),
                  full(g), full(b)],
        out_specs=[pl.BlockSpec((tm, D), row), pl.BlockSpec((tm, D), row),
                   pl.BlockSpec((D, tm), lambda i: (0, i))],
        out_shape=[jax.ShapeDtypeStruct((T, D), F32), jax.ShapeDtypeStruct((T, D), BF16),
                   jax.ShapeDtypeStruct((D, T), BF16)],
        compiler_params=_cparams("parallel"),
        name="merge_ln1",
    )(x2, ya, yb, wg, bg, wa, wb, wo, g, b)


def _peer_scores_kernel(h_ref, wq_ref, keys_ref, s_ref):
    q = _dot(h_ref[...], wq_ref[...]).astype(BF16)
    half = PEER_KEY_DIM // 2
    for g in range(2 * PEER_HEADS):
        s_ref[g] = _nt_dot(keys_ref[g], q[:, g * half:(g + 1) * half])


def _peer_scores(hb, wq, keys):
    T, D = hb.shape
    tm = TOKEN_TILE
    G = 2 * PEER_HEADS
    return pl.pallas_call(
        _peer_scores_kernel,
        grid=(T // tm,),
        in_specs=[pl.BlockSpec((tm, D), lambda i: (i, 0)),
                  pl.BlockSpec(wq.shape, lambda i: (0, 0)),
                  pl.BlockSpec(keys.shape, lambda i: (0, 0, 0))],
        out_specs=pl.BlockSpec((G, PEER_N_KEYS, tm), lambda i: (0, 0, i)),
        out_shape=jax.ShapeDtypeStruct((G, PEER_N_KEYS, T), F32),
        compiler_params=_cparams("parallel"),
        name="peer_scores",
    )(hb, wq, keys)


def _top_values(v, k, top_ref, slot):
    rank = jnp.full(v.shape, float(k), F32)
    for r in range(k):
        m = jnp.max(v, axis=0, keepdims=True)
        top_ref[slot, pl.ds(r, 1), :] = m
        hit = v == m
        rank = jnp.where(hit, float(r), rank)
        v = jnp.where(hit, -jnp.inf, v)
    return rank


def _peer_select_kernel(s_ref, cnt_ref, ca_ref, rb_ref, eb_ref, cand_ref, top_ref, ra_ref, cbr_ref):
    tl = s_ref.shape[2]
    cand_ref[...] = jnp.full(cand_ref.shape, -jnp.inf, F32)
    row = lambda slot, r: top_ref[slot, pl.ds(r, 1), :]
    for h in range(PEER_HEADS):
        rb_ref[h] = _top_values(s_ref[2 * h + 1], PEER_TOPK, top_ref, 1).astype(BF16)
        eb_ref[h] = jnp.exp(s_ref[2 * h + 1] - row(1, 0)).astype(BF16)
        ra_ref[...] = _top_values(s_ref[2 * h], PEER_TOPK, top_ref, 0)
        for c, (r, s) in enumerate(_PEER_CAND):
            cand_ref[pl.ds(c, 1), :] = row(0, r) + row(1, s)
        cand = cand_ref[...]
        rest = cand
        for _ in range(PEER_TOPK):
            tau = jnp.max(rest, axis=0, keepdims=True)
            rest = jnp.where(rest == tau, -jnp.inf, rest)
        z = jnp.sum(jnp.where(cand >= tau, jnp.exp(cand - cand[0:1, :]), 0.0), axis=0, keepdims=True)
        c = 0
        for r in range(PEER_TOPK):
            n = jnp.zeros((1, tl), F32)
            while c < len(_PEER_CAND) and _PEER_CAND[c][0] == r:
                n = n + jnp.where(cand[c:c + 1, :] >= tau, 1.0, 0.0)
                c += 1
            cbr_ref[pl.ds(r, 1), :] = n
        ra = ra_ref[...]
        cnt = jnp.zeros(ra.shape, F32)
        for r in range(PEER_TOPK):
            cnt = jnp.where(ra == float(r), cbr_ref[pl.ds(r, 1), :], cnt)
        cnt_ref[h] = cnt
        ca_ref[h] = jnp.exp(s_ref[2 * h] - row(0, 0)) / z


def _peer_select(sT):
    G, N, T = sT.shape
    tl = PEER_SEL_TILE
    ospec = pl.BlockSpec((PEER_HEADS, N, tl), lambda i: (0, 0, i))
    shape = (PEER_HEADS, N, T)
    return pl.pallas_call(
        _peer_select_kernel,
        grid=(T // tl,),
        in_specs=[pl.BlockSpec((G, N, tl), lambda i: (0, 0, i))],
        out_specs=[ospec, ospec, ospec, ospec],
        out_shape=[jax.ShapeDtypeStruct(shape, F32), jax.ShapeDtypeStruct(shape, F32),
                   jax.ShapeDtypeStruct(shape, BF16), jax.ShapeDtypeStruct(shape, BF16)],
        scratch_shapes=[pltpu.VMEM((_PEER_CAND_ROWS, tl), F32), pltpu.VMEM((2, PEER_TOPK, tl), F32),
                        pltpu.VMEM((N, tl), F32), pltpu.VMEM((PEER_TOPK, tl), F32)],
        compiler_params=_cparams("parallel"),
        name="peer_select",
    )(sT)


def _gelu(v):
    return 0.5 * v * (1.0 + lax.erf(v * (2.0 ** -0.5)))


def _peer_main_kernel(hT_ref, u_ref, vT_ref, cnt_ref, ca_ref, rb_ref, eb_ref, o_ref, wg_ref):
    e = pl.program_id(1)
    N = PEER_N_KEYS
    EH = PEER_EXPERT_HALF
    chains = PEER_EXPERT_TILE // EH
    per = EH // N
    tm = hT_ref.shape[1]

    @pl.when(e == 0)
    def _():
        o_ref[...] = jnp.zeros(o_ref.shape, o_ref.dtype)

    hT = hT_ref[...]
    contrib = None

    def value_matmul(c, contrib):
        rows = slice(c * EH, (c + 1) * EH)
        d = _dot(vT_ref[:, rows], wg_ref[rows, :])
        return d if contrib is None else contrib + d

    pre = _dot(u_ref[0:EH, :], hT)
    for c in range(chains):
        cur = pre
        if c + 1 < chains:
            pre = _dot(u_ref[(c + 1) * EH:(c + 2) * EH, :], hT)
        for sub in range(per):
            i1 = (e * chains + c) * per + sub
            w = jnp.zeros((N, tm), BF16)
            for h in range(PEER_HEADS):
                cnt = cnt_ref[h, pl.ds(i1, 1), :].astype(BF16)
                ca = ca_ref[h, pl.ds(i1, 1), :].astype(BF16)
                w = w + jnp.where(rb_ref[h] < cnt, eb_ref[h], jnp.zeros((), BF16)) * ca
            r0 = c * EH + sub * N
            wg_ref[r0:r0 + N, :] = w * _gelu(cur[sub * N:(sub + 1) * N, :]).astype(BF16)
            if sub == 0 and c > 0:
                contrib = value_matmul(c - 1, contrib)
    contrib = value_matmul(chains - 1, contrib)
    o_ref[...] += contrib


def _peer_main(hT, u, vT, cnt, ca, rb, eb):
    D, T = hT.shape
    NE = u.shape[0]
    tm = PEER_TOKEN_TILE
    te = PEER_EXPERT_TILE
    stat = pl.BlockSpec((PEER_HEADS, PEER_N_KEYS, tm), lambda i, e: (0, 0, i))
    return pl.pallas_call(
        _peer_main_kernel,
        grid=(T // tm, NE // te),
        in_specs=[pl.BlockSpec((D, tm), lambda i, e: (0, i)),
                  pl.BlockSpec((te, D), lambda i, e: (e, 0)),
                  pl.BlockSpec((D, te), lambda i, e: (0, e)),
                  stat, stat, stat, stat],
        out_specs=pl.BlockSpec((D, tm), lambda i, e: (0, i)),
        out_shape=jax.ShapeDtypeStruct((D, T), F32),
        scratch_shapes=[pltpu.VMEM((te, tm), BF16)],
        compiler_params=_cparams("parallel", "arbitrary"),
        name="peer_main",
    )(hT, u, vT, cnt, ca, rb, eb)


def _final_kernel(h_ref, peerT_ref, p_ref, wg_ref, bg_ref, wp_ref, g_ref, b_ref, o_ref, *, alpha):
    r = alpha * h_ref[...] + peerT_ref[...].T
    gate = _sigmoid(_dot(r.astype(BF16), wg_ref[...]) + bg_ref[...])
    ple = gate * _dot(p_ref[...].astype(BF16), wp_ref[...])
    o_ref[...] = _layer_norm(r + ple, g_ref[...], b_ref[...])


def _final(h, peer_t, p2, wg, bg, wp, g, b, alpha):
    T, D = h.shape
    tm = TOKEN_TILE
    row = lambda i: (i, 0)
    full = lambda a: pl.BlockSpec(a.shape, lambda i: (0, 0))
    return pl.pallas_call(
        functools.partial(_final_kernel, alpha=alpha),
        grid=(T // tm,),
        in_specs=[pl.BlockSpec((tm, D), row), pl.BlockSpec((D, tm), lambda i: (0, i)),
                  pl.BlockSpec((tm, p2.shape[1]), row), full(wg), full(bg), full(wp), full(g), full(b)],
        out_specs=pl.BlockSpec((tm, D), row),
        out_shape=jax.ShapeDtypeStruct((T, D), F32),
        compiler_params=_cparams("parallel"),
        name="ple_ln2",
    )(h, peer_t, p2, wg, bg, wp, g, b)


def _rope_tables(seq, rot_dim, theta):
    inv = jnp.power(jnp.float32(theta), -jnp.arange(0, rot_dim, 2, dtype=F32) / rot_dim)
    ang = jnp.arange(seq).astype(F32)[:, None] * inv[None, :]
    ang = jnp.concatenate([ang, ang], axis=-1)
    return jnp.cos(ang), jnp.sin(ang)


def _rotate_half_cols(w):
    r = w.shape[-1]
    return jnp.concatenate([-w[..., r // 2:], w[..., :r // 2]], axis=-1)


def _pad_last(a, before, total):
    pad = [(0, 0)] * (a.ndim - 1) + [(before, total - before - a.shape[-1])]
    return jnp.pad(a, pad)


def kernel(x, p, w_in, b_gates, mla_q_norm, w_q_up, mla_kv_norm, w_kv_up, w_branch_moba, w_branch_mla, w_out, ln1_g, ln1_b, peer_w_q, peer_sub_keys, peer_u, peer_v, w_ple_proj, w_ple_gate, b_ple_gate, ln2_g, ln2_b):
    B, S, D = x.shape
    depth = w_in.shape[0]
    T = B * S
    alpha = (2.0 * depth) ** 0.25
    H, dh = MOBA_HEADS, MOBA_HEAD_DIM

    cos_p, sin_p = _rope_tables(S, MOBA_ROT, MOBA_THETA)
    cos_a = jnp.tile(jnp.concatenate([cos_p, jnp.ones((S, dh - MOBA_ROT), F32)], axis=1), (1, LANES // dh))
    sin_a = jnp.tile(jnp.concatenate([sin_p, jnp.zeros((S, dh - MOBA_ROT), F32)], axis=1), (1, LANES // dh))
    cos_m, sin_m = _rope_tables(S, MLA_ROPE, MLA_THETA)
    cos_b = jnp.concatenate([jnp.ones((S, MLA_NOPE), F32), cos_m,
                             jnp.zeros((S, MLA_PAD - MLA_QK), F32)], axis=1)
    sin_b = _pad_last(sin_m, MLA_NOPE, MLA_PAD)

    o_k, o_v, o_cq = MOBA_W, 2 * MOBA_W, 3 * MOBA_W
    o_ckv = o_cq + MLA_Q_LORA
    o_kr = o_ckv + MLA_KV_LORA
    o_g = o_kr + MLA_ROPE

    def moba_rot(w):
        w3 = w.reshape(D, H, dh)
        return _pad_last(_rotate_half_cols(w3[..., :MOBA_ROT]), 0, dh).reshape(D, H * dh)

    x2 = x.reshape(T, D)
    for i in range(depth):
        w = w_in[i]
        wq, wk, wv = w[:, :o_k], w[:, o_k:o_v], w[:, o_v:o_cq]
        w_moba = jnp.concatenate([wq, moba_rot(wq), wk, moba_rot(wk), wv], axis=1).astype(BF16)
        q_a, k_a, v_a, kmean = _moba_proj(x2, w_moba, cos_a, sin_a, S)

        wkr = w[:, o_kr:o_g]
        wkr2 = jnp.concatenate([_pad_last(wkr, MLA_NOPE, MLA_PAD),
                                _pad_last(_rotate_half_cols(wkr), MLA_NOPE, MLA_PAD)], axis=1).astype(BF16)
        wqu = w_q_up[i].reshape(MLA_Q_LORA, MLA_HEADS, MLA_QK)
        wq_pad = _pad_last(wqu, 0, MLA_PAD).reshape(MLA_Q_LORA, -1).astype(BF16)
        wqr_pad = _pad_last(_rotate_half_cols(wqu[..., MLA_NOPE:]), MLA_NOPE, MLA_PAD)
        wqr_pad = wqr_pad.reshape(MLA_Q_LORA, -1).astype(BF16)
        wkv = w_kv_up[i].reshape(MLA_KV_LORA, MLA_HEADS, MLA_NOPE + MLA_V)
        wkn_pad = _pad_last(wkv[..., :MLA_NOPE], 0, MLA_PAD).reshape(MLA_KV_LORA, -1).astype(BF16)
        wvv = wkv[..., MLA_NOPE:].reshape(MLA_KV_LORA, -1).astype(BF16)
        q_m, k_m, v_m = _mla_proj(
            x2, w[:, o_cq:o_ckv].astype(BF16), w[:, o_ckv:o_kr].astype(BF16), wkr2,
            mla_q_norm[i][None, :], mla_kv_norm[i][None, :], wq_pad, wqr_pad, wkn_pad, wvv, cos_b, sin_b, S)

        y_a = _moba_attn(q_a, k_a, v_a.reshape(B, H, dh, S), kmean.reshape(T // MOBA_BLOCK, MOBA_W), B, S)
        y_b = _mla_attn(q_m, k_m, v_m.reshape(B, MLA_HEADS, MLA_V, S), B, S)

        h, hb, hT = _merge(x2, y_a, y_b, w[:, o_g:].astype(BF16), b_gates[i].reshape(1, -1),
                       w_branch_moba[i].astype(BF16), w_branch_mla[i].astype(BF16), w_out[i].astype(BF16),
                       ln1_g[i][None, :], ln1_b[i][None, :], alpha)

        keys = peer_sub_keys[i].reshape(2 * PEER_HEADS, PEER_N_KEYS, PEER_KEY_DIM // 2).astype(BF16)
        sT = _peer_scores(hb, peer_w_q[i].astype(BF16), keys)
        cnt, ca, rb, eb = _peer_select(sT)
        peer_t = _peer_main(hT, peer_u[i].astype(BF16), peer_v[i].T.astype(BF16), cnt, ca, rb, eb)

        x2 = _final(h, peer_t, p[i].reshape(T, -1), w_ple_gate[i].astype(BF16), b_ple_gate[i][None, :],
                    w_ple_proj[i].astype(BF16), ln2_g[i][None, :], ln2_b[i][None, :], alpha)
    return x2.reshape(B, S, D)
```

```python
import functools

import jax
import jax.numpy as jnp
from jax import lax
from jax.experimental import pallas as pl
from jax.experimental.pallas import tpu as pltpu

F32 = jnp.float32
BF16 = jnp.bfloat16

LANES = 128
MOBA_HEADS = 8
MOBA_HEAD_DIM = 64
MOBA_BLOCK = 256
MOBA_TOPK = 3
MOBA_ROT = MOBA_HEAD_DIM // 4
MOBA_THETA = 500000.0
MOBA_W = MOBA_HEADS * MOBA_HEAD_DIM
MLA_HEADS = 8
MLA_Q_LORA = 512
MLA_KV_LORA = 256
MLA_NOPE = 64
MLA_ROPE = 32
MLA_V = 64
MLA_QK = MLA_NOPE + MLA_ROPE
MLA_PAD = LANES
MLA_THETA = 10000.0
PEER_HEADS = 8
PEER_N_KEYS = 128
PEER_KEY_DIM = 256
PEER_TOPK = 16
NEG_INF = -1e30
LOG2_E = 1.4426950408889634
LN_EPS = 1e-5
RMS_EPS = 1e-6

V7X_VMEM_LIMIT_BYTES = 48 * 1024 * 1024

TOKEN_TILE = 512
MOBA_GROUP = 4
MLA_TILE = 512
MLA_GROUP = 4
PEER_SEL_TILE = 256
PEER_TOKEN_TILE = 512
PEER_EXPERT_TILE = 2048
PEER_EXPERT_HALF = 512

_PEER_CAND = tuple((r, s) for r in range(PEER_TOPK) for s in range(PEER_TOPK)
                   if (r + 1) * (s + 1) <= PEER_TOPK)
_PEER_CAND_ROWS = -(-len(_PEER_CAND) // 8) * 8


def _cparams(*sem):
    return pltpu.CompilerParams(dimension_semantics=sem, vmem_limit_bytes=V7X_VMEM_LIMIT_BYTES)


def _nt_dot(a, b, precision=None):
    return lax.dot_general(a, b, (((1,), (1,)), ((), ())), preferred_element_type=F32,
                           precision=precision)


def _dot(a, b):
    return jnp.dot(a, b, preferred_element_type=F32)


def _layer_norm(v, g, b):
    mu = jnp.mean(v, axis=-1, keepdims=True)
    d = v - mu
    var = jnp.mean(d * d, axis=-1, keepdims=True)
    return d * lax.rsqrt(var + LN_EPS) * g + b


def _rms_norm(v, g):
    return v * lax.rsqrt(jnp.mean(v * v, axis=-1, keepdims=True) + RMS_EPS) * g


def _sigmoid(v):
    return 1.0 / (1.0 + jnp.exp(-v))


def _tile_lanes(v, n):
    return jnp.concatenate([v] * n, axis=1)


def _moba_proj_kernel(x_ref, w_ref, cos_ref, sin_ref, q_ref, k_ref, v_ref, km_ref):
    xb = x_ref[...].astype(BF16)
    W = MOBA_W
    cos = _tile_lanes(cos_ref[...], W // LANES)
    sin = _tile_lanes(sin_ref[...], W // LANES)
    q = _dot(xb, w_ref[:, 0:W]) * cos + _dot(xb, w_ref[:, W:2 * W]) * sin
    k = _dot(xb, w_ref[:, 2 * W:3 * W]) * cos + _dot(xb, w_ref[:, 3 * W:4 * W]) * sin
    v = _dot(xb, w_ref[:, 4 * W:5 * W])
    q_ref[...] = q
    k_ref[...] = k.astype(BF16)
    v_ref[0] = v.T.astype(BF16)
    nblk = k.shape[0] // MOBA_BLOCK
    km_ref[0] = jnp.sum(k.reshape(nblk, MOBA_BLOCK, W), axis=1) * (1.0 / MOBA_BLOCK)


def _moba_proj(x2, w, cos, sin, seq):
    T, D = x2.shape
    tm = TOKEN_TILE
    nblk = tm // MOBA_BLOCK
    ps = seq // tm
    row = lambda i: (i, 0)
    pos = lambda i: (i % ps, 0)
    return pl.pallas_call(
        _moba_proj_kernel,
        grid=(T // tm,),
        in_specs=[pl.BlockSpec((tm, D), row),
                  pl.BlockSpec(w.shape, lambda i: (0, 0)),
                  pl.BlockSpec((tm, LANES), pos),
                  pl.BlockSpec((tm, LANES), pos)],
        out_specs=[pl.BlockSpec((tm, MOBA_W), row),
                   pl.BlockSpec((tm, MOBA_W), row),
                   pl.BlockSpec((1, MOBA_W, tm), lambda i: (i // ps, 0, i % ps)),
                   pl.BlockSpec((1, nblk, MOBA_W), lambda i: (i, 0, 0))],
        out_shape=[jax.ShapeDtypeStruct((T, MOBA_W), F32),
                   jax.ShapeDtypeStruct((T, MOBA_W), BF16),
                   jax.ShapeDtypeStruct((T // seq, MOBA_W, seq), BF16),
                   jax.ShapeDtypeStruct((T // tm, nblk, MOBA_W), F32)],
        compiler_params=_cparams("parallel"),
        name="moba_proj",
    )(x2, w, cos, sin)


def _mla_proj_kernel(x_ref, wcq_ref, wckv_ref, wkr_ref, gq_ref, gkv_ref, wq_ref, wqr_ref,
                     wkn_ref, wv_ref, cos_ref, sin_ref, qm_ref, km_ref, v_ref):
    xb = x_ref[...].astype(BF16)
    cos1 = cos_ref[...]
    sin1 = sin_ref[...]
    cos = _tile_lanes(cos1, MLA_HEADS)
    sin = _tile_lanes(sin1, MLA_HEADS)
    scale = MLA_QK ** -0.5 * LOG2_E
    cq = _rms_norm(_dot(xb, wcq_ref[...]), gq_ref[...]).astype(BF16)
    q = _dot(cq, wq_ref[...]) * cos + _dot(cq, wqr_ref[...]) * sin
    qm_ref[...] = (q * scale).astype(BF16)
    ckv = _rms_norm(_dot(xb, wckv_ref[...]), gkv_ref[...]).astype(BF16)
    kr2 = _dot(xb, wkr_ref[...])
    kr = kr2[:, :MLA_PAD] * cos1 + kr2[:, MLA_PAD:] * sin1
    km_ref[...] = (_dot(ckv, wkn_ref[...]) + _tile_lanes(kr, MLA_HEADS)).astype(BF16)
    v_ref[0] = _dot(ckv, wv_ref[...]).T.astype(BF16)


def _mla_proj(x2, wcq, wckv, wkr2, gq, gkv, wq, wqr, wkn, wv, cos, sin, seq):
    T, D = x2.shape
    tm = TOKEN_TILE
    ps = seq // tm
    row = lambda i: (i, 0)
    pos = lambda i: (i % ps, 0)
    full = lambda a: pl.BlockSpec(a.shape, lambda i: (0, 0))
    HP = MLA_HEADS * MLA_PAD
    HV = MLA_HEADS * MLA_V
    return pl.pallas_call(
        _mla_proj_kernel,
        grid=(T // tm,),
        in_specs=[pl.BlockSpec((tm, D), row), full(wcq), full(wckv), full(wkr2), full(gq), full(gkv),
                  full(wq), full(wqr), full(wkn), full(wv),
                  pl.BlockSpec((tm, MLA_PAD), pos), pl.BlockSpec((tm, MLA_PAD), pos)],
        out_specs=[pl.BlockSpec((tm, HP), row), pl.BlockSpec((tm, HP), row),
                   pl.BlockSpec((1, HV, tm), lambda i: (i // ps, 0, i % ps))],
        out_shape=[jax.ShapeDtypeStruct((T, HP), BF16), jax.ShapeDtypeStruct((T, HP), BF16),
                   jax.ShapeDtypeStruct((T // seq, HV, seq), BF16)],
        compiler_params=_cparams("parallel"),
        name="mla_proj",
    )(x2, wcq, wckv, wkr2, gq, gkv, wq, wqr, wkn, wv, cos, sin)


def _softmax_update(sT, vT, m_ref, l_ref, acc_ref, g, rows):
    m_old = m_ref[g]
    m_new = jnp.maximum(m_old, jnp.max(sT, axis=0, keepdims=True))
    p = jnp.exp2(sT - m_new)
    alpha = jnp.exp2(m_old - m_new)
    m_ref[g] = m_new
    l_ref[g] = alpha * l_ref[g] + jnp.sum(p, axis=0, keepdims=True)
    acc_ref[rows, :] = alpha * acc_ref[rows, :] + _dot(vT, p.astype(BF16))


def _attn_pipeline(qi, t, groups, dv, score_own, score_past, value_block, m_ref, l_ref, acc_ref):
    m_ref[...] = jnp.full(m_ref.shape, NEG_INF, F32)
    l_ref[...] = jnp.zeros(l_ref.shape, F32)
    acc_ref[...] = jnp.zeros(acc_ref.shape, F32)

    def consume(scores, blk):
        for g in range(groups):
            _softmax_update(scores[g], value_block(blk, g), m_ref, l_ref, acc_ref, g,
                            slice(g * dv, (g + 1) * dv))

    def body(kj, scores):
        nxt = tuple(score_past(kj, g) for g in range(groups))
        consume(scores, jnp.where(kj == 0, qi, kj - 1))
        return nxt

    last = lax.fori_loop(0, qi, body, tuple(score_own(g) for g in range(groups)))
    consume(last, jnp.maximum(qi - 1, 0))


def _attn_serial(qi, t, groups, dv, score_own, score_past, value_block, m_ref, l_ref, acc_ref):
    m_ref[...] = jnp.full(m_ref.shape, NEG_INF, F32)
    l_ref[...] = jnp.zeros(l_ref.shape, F32)
    acc_ref[...] = jnp.zeros(acc_ref.shape, F32)
    for g in range(groups):
        _softmax_update(score_own(g), value_block(qi, g), m_ref, l_ref, acc_ref, g, slice(g * dv, (g + 1) * dv))

    def body(kj, c):
        for g in range(groups):
            _softmax_update(score_past(kj, g), value_block(kj, g), m_ref, l_ref, acc_ref, g,
                            slice(g * dv, (g + 1) * dv))
        return c

    lax.fori_loop(0, qi, body, 0)


def _causal_bias(t):
    kpos = lax.broadcasted_iota(jnp.int32, (t, t), 0)
    qpos = lax.broadcasted_iota(jnp.int32, (t, t), 1)
    return jnp.where(kpos <= qpos, 0.0, NEG_INF).astype(F32)


def _attn_finish(o_ref, l_ref, acc_ref, groups, dv):
    for g in range(groups):
        rows = slice(g * dv, (g + 1) * dv)
        acc_ref[rows, :] = acc_ref[rows, :] / l_ref[g]
    o_ref[...] = acc_ref[...].T.astype(o_ref.dtype)


def _moba_attn_kernel(q_ref, k_ref, vT_ref, km_ref, o_ref, sel_ref, qb_ref, m_ref, l_ref, acc_ref):
    t = MOBA_BLOCK
    G = MOBA_GROUP
    dh = MOBA_HEAD_DIM
    qi = pl.program_id(2)
    nb = km_ref.shape[0]
    q_all = q_ref[...]
    lane = lax.broadcasted_iota(jnp.int32, q_all.shape, 1)
    blk = lax.broadcasted_iota(jnp.int32, (nb, t), 0)
    valid = blk < qi
    for g in range(G):
        qg = jnp.where(jnp.logical_and(lane >= g * dh, lane < (g + 1) * dh), q_all, 0.0)
        gate = _nt_dot(km_ref[...], qg, precision=lax.Precision.HIGHEST)
        gate = jnp.where(valid, gate, NEG_INF)
        rank = jnp.zeros((nb, t), F32)
        for n in range(nb):
            row = gate[n:n + 1, :]
            before = jnp.logical_or(row > gate, jnp.logical_and(row == gate, blk > n))
            rank = rank + jnp.where(before, 1.0, 0.0)
        chosen = jnp.logical_and(valid, rank < float(MOBA_TOPK))
        sel_ref[g] = jnp.where(chosen, 0.0, NEG_INF)
        qb_ref[g] = (qg * (dh ** -0.5 * LOG2_E)).astype(BF16)

    def score_own(g):
        q0 = pl.multiple_of(qi * t, t)
        return _nt_dot(k_ref[pl.ds(q0, t), :], qb_ref[g]) + _causal_bias(t)

    def score_past(kj, g):
        k0 = pl.multiple_of(kj * t, t)
        return _nt_dot(k_ref[pl.ds(k0, t), :], qb_ref[g]) + sel_ref[g, pl.ds(kj, 1), :]

    def value_block(blk_id, g):
        return vT_ref[0, g, :, pl.ds(pl.multiple_of(blk_id * t, t), t)]

    _attn_pipeline(qi, t, G, dh, score_own, score_past, value_block, m_ref, l_ref, acc_ref)
    _attn_finish(o_ref, l_ref, acc_ref, G, dh)


def _moba_attn(q, k, vT, kmean, B, S):
    H, dh, G = MOBA_HEADS, MOBA_HEAD_DIM, MOBA_GROUP
    t = MOBA_BLOCK
    nb = S // t
    gw = G * dh
    return pl.pallas_call(
        _moba_attn_kernel,
        grid=(B, H // G, nb),
        in_specs=[pl.BlockSpec((t, gw), lambda b, h, i: (b * nb + i, h)),
                  pl.BlockSpec((S, gw), lambda b, h, i: (b, h)),
                  pl.BlockSpec((1, G, dh, S), lambda b, h, i: (b, h, 0, 0)),
                  pl.BlockSpec((nb, gw), lambda b, h, i: (b, h))],
        out_specs=pl.BlockSpec((t, gw), lambda b, h, i: (b * nb + i, h)),
        out_shape=jax.ShapeDtypeStruct((B * S, H * dh), BF16),
        scratch_shapes=[pltpu.VMEM((G, nb, t), F32), pltpu.VMEM((G, t, gw), BF16),
                        pltpu.VMEM((G, 1, t), F32), pltpu.VMEM((G, 1, t), F32),
                        pltpu.VMEM((gw, t), F32)],
        compiler_params=_cparams("parallel", "parallel", "arbitrary"),
        name="moba_attn",
    )(q, k, vT, kmean)


def _mla_attn_kernel(q_ref, k_ref, vT_ref, o_ref, m_ref, l_ref, acc_ref):
    t = MLA_TILE
    G = MLA_GROUP
    P = MLA_PAD
    dv = MLA_V
    qi = pl.program_id(2)

    def score(blk_id, g):
        cols = slice(g * P, (g + 1) * P)
        k0 = pl.multiple_of(blk_id * t, t)
        return _nt_dot(k_ref[pl.ds(k0, t), cols], q_ref[:, cols])

    def score_own(g):
        return score(qi, g) + _causal_bias(t)

    def value_block(blk_id, g):
        return vT_ref[0, g, :, pl.ds(pl.multiple_of(blk_id * t, t), t)]

    _attn_serial(qi, t, G, dv, score_own, score, value_block, m_ref, l_ref, acc_ref)
    _attn_finish(o_ref, l_ref, acc_ref, G, dv)


def _mla_attn(q, k, vT, B, S):
    H, G, P, dv = MLA_HEADS, MLA_GROUP, MLA_PAD, MLA_V
    t = MLA_TILE
    nq = S // t
    return pl.pallas_call(
        _mla_attn_kernel,
        grid=(B, H // G, nq),
        in_specs=[pl.BlockSpec((t, G * P), lambda b, h, i: (b * nq + i, h)),
                  pl.BlockSpec((S, G * P), lambda b, h, i: (b, h)),
                  pl.BlockSpec((1, G, dv, S), lambda b, h, i: (b, h, 0, 0))],
        out_specs=pl.BlockSpec((t, G * dv), lambda b, h, i: (b * nq + i, h)),
        out_shape=jax.ShapeDtypeStruct((B * S, H * dv), BF16),
        scratch_shapes=[pltpu.VMEM((G, 1, t), F32), pltpu.VMEM((G, 1, t), F32),
                        pltpu.VMEM((G * dv, t), F32)],
        compiler_params=_cparams("parallel", "parallel", "arbitrary"),
        name="mla_attn",
    )(q, k, vT)


def _merge_kernel(x_ref, ya_ref, yb_ref, wg_ref, bg_ref, wa_ref, wb_ref, wo_ref, g_ref, b_ref,
                  wq_ref, keys_ref, h_ref, hT_ref, s_ref, *, alpha):
    x = x_ref[...]
    D = x.shape[1]
    xb = x.astype(BF16)
    ga = _sigmoid(_dot(xb, wg_ref[:, :D]) + bg_ref[:, :D])
    gb = _sigmoid(_dot(xb, wg_ref[:, D:]) + bg_ref[:, D:])
    merged = ga * _dot(ya_ref[...], wa_ref[...]) + gb * _dot(yb_ref[...], wb_ref[...])
    mix = _dot(merged.astype(BF16), wo_ref[...])
    h = _layer_norm(alpha * x + mix, g_ref[...], b_ref[...])
    h_ref[...] = h
    hT_ref[...] = h.T.astype(BF16)
    q = _dot(h.astype(BF16), wq_ref[...]).astype(BF16)
    half = PEER_KEY_DIM // 2
    for g in range(2 * PEER_HEADS):
        s_ref[g] = _nt_dot(keys_ref[g], q[:, g * half:(g + 1) * half])


def _merge(x2, ya, yb, wg, bg, wa, wb, wo, g, b, wq, keys, alpha):
    T, D = x2.shape
    tm = TOKEN_TILE
    G = 2 * PEER_HEADS
    row = lambda i: (i, 0)
    full = lambda a: pl.BlockSpec(a.shape, lambda i: (0,) * a.ndim)
    return pl.pallas_call(
        functools.partial(_merge_kernel, alpha=alpha),
        grid=(T // tm,),
        in_specs=[pl.BlockSpec((tm, D), row), pl.BlockSpec((tm, ya.shape[1]), row),
                  pl.BlockSpec((tm, yb.shape[1]), row), full(wg), full(bg), full(wa), full(wb), full(wo),
                  full(g), full(b), full(wq), full(keys)],
        out_specs=[pl.BlockSpec((tm, D), row), pl.BlockSpec((D, tm), lambda i: (0, i)),
                   pl.BlockSpec((G, PEER_N_KEYS, tm), lambda i: (0, 0, i))],
        out_shape=[jax.ShapeDtypeStruct((T, D), F32), jax.ShapeDtypeStruct((D, T), BF16),
                   jax.ShapeDtypeStruct((G, PEER_N_KEYS, T), F32)],
        compiler_params=_cparams("parallel"),
        name="merge_ln1_peer_scores",
    )(x2, ya, yb, wg, bg, wa, wb, wo, g, b, wq, keys)


def _top_values(v, k, top_ref, slot):
    rank = jnp.full(v.shape, float(k), F32)
    for r in range(k):
        m = jnp.max(v, axis=0, keepdims=True)
        top_ref[slot, pl.ds(r, 1), :] = m
        hit = v == m
        rank = jnp.where(hit, float(r), rank)
        v = jnp.where(hit, -jnp.inf, v)
    return rank


def _peer_select_kernel(s_ref, cnt_ref, ca_ref, rb_ref, eb_ref, cand_ref, top_ref, ra_ref, cbr_ref):
    tl = s_ref.shape[2]
    cand_ref[...] = jnp.full(cand_ref.shape, -jnp.inf, F32)
    row = lambda slot, r: top_ref[slot, pl.ds(r, 1), :]
    for h in range(PEER_HEADS):
        rb_ref[h] = _top_values(s_ref[2 * h + 1], PEER_TOPK, top_ref, 1).astype(BF16)
        eb_ref[h] = jnp.exp(s_ref[2 * h + 1] - row(1, 0)).astype(BF16)
        ra_ref[...] = _top_values(s_ref[2 * h], PEER_TOPK, top_ref, 0)
        for c, (r, s) in enumerate(_PEER_CAND):
            cand_ref[pl.ds(c, 1), :] = row(0, r) + row(1, s)
        cand = cand_ref[...]
        rest = cand
        for _ in range(PEER_TOPK):
            tau = jnp.max(rest, axis=0, keepdims=True)
            rest = jnp.where(rest == tau, -jnp.inf, rest)
        z = jnp.sum(jnp.where(cand >= tau, jnp.exp(cand - cand[0:1, :]), 0.0), axis=0, keepdims=True)
        c = 0
        for r in range(PEER_TOPK):
            n = jnp.zeros((1, tl), F32)
            while c < len(_PEER_CAND) and _PEER_CAND[c][0] == r:
                n = n + jnp.where(cand[c:c + 1, :] >= tau, 1.0, 0.0)
                c += 1
            cbr_ref[pl.ds(r, 1), :] = n
        ra = ra_ref[...]
        cnt = jnp.zeros(ra.shape, F32)
        for r in range(PEER_TOPK):
            cnt = jnp.where(ra == float(r), cbr_ref[pl.ds(r, 1), :], cnt)
        cnt_ref[h] = cnt
        ca_ref[h] = jnp.exp(s_ref[2 * h] - row(0, 0)) / z


def _peer_select(sT):
    G, N, T = sT.shape
    tl = PEER_SEL_TILE
    ospec = pl.BlockSpec((PEER_HEADS, N, tl), lambda i: (0, 0, i))
    shape = (PEER_HEADS, N, T)
    return pl.pallas_call(
        _peer_select_kernel,
        grid=(T // tl,),
        in_specs=[pl.BlockSpec((G, N, tl), lambda i: (0, 0, i))],
        out_specs=[ospec, ospec, ospec, ospec],
        out_shape=[jax.ShapeDtypeStruct(shape, F32), jax.ShapeDtypeStruct(shape, F32),
                   jax.ShapeDtypeStruct(shape, BF16), jax.ShapeDtypeStruct(shape, BF16)],
        scratch_shapes=[pltpu.VMEM((_PEER_CAND_ROWS, tl), F32), pltpu.VMEM((2, PEER_TOPK, tl), F32),
                        pltpu.VMEM((N, tl), F32), pltpu.VMEM((PEER_TOPK, tl), F32)],
        compiler_params=_cparams("parallel"),
        name="peer_select",
    )(sT)


def _gelu(v):
    return 0.5 * v * (1.0 + lax.erf(v * (2.0 ** -0.5)))


def _peer_main_kernel(hT_ref, u_ref, vT_ref, cnt_ref, ca_ref, rb_ref, eb_ref, o_ref, wg_ref):
    e = pl.program_id(1)
    N = PEER_N_KEYS
    EH = PEER_EXPERT_HALF
    chains = PEER_EXPERT_TILE // EH
    per = EH // N
    tm = hT_ref.shape[1]

    @pl.when(e == 0)
    def _():
        o_ref[...] = jnp.zeros(o_ref.shape, o_ref.dtype)

    hT = hT_ref[...]
    contrib = None

    def value_matmul(c, contrib):
        rows = slice(c * EH, (c + 1) * EH)
        d = _dot(vT_ref[:, rows], wg_ref[rows, :])
        return d if contrib is None else contrib + d

    pre = _dot(u_ref[0:EH, :], hT)
    for c in range(chains):
        cur = pre
        if c + 1 < chains:
            pre = _dot(u_ref[(c + 1) * EH:(c + 2) * EH, :], hT)
        for sub in range(per):
            i1 = (e * chains + c) * per + sub
            w = jnp.zeros((N, tm), BF16)
            for h in range(PEER_HEADS):
                cnt = cnt_ref[h, pl.ds(i1, 1), :].astype(BF16)
                ca = ca_ref[h, pl.ds(i1, 1), :].astype(BF16)
                w = w + jnp.where(rb_ref[h] < cnt, eb_ref[h], jnp.zeros((), BF16)) * ca
            r0 = c * EH + sub * N
            wg_ref[r0:r0 + N, :] = w * _gelu(cur[sub * N:(sub + 1) * N, :]).astype(BF16)
            if sub == 0 and c > 0:
                contrib = value_matmul(c - 1, contrib)
    contrib = value_matmul(chains - 1, contrib)
    o_ref[...] += contrib


def _peer_main(hT, u, vT, cnt, ca, rb, eb):
    D, T = hT.shape
    NE = u.shape[0]
    tm = PEER_TOKEN_TILE
    te = PEER_EXPERT_TILE
    stat = pl.BlockSpec((PEER_HEADS, PEER_N_KEYS, tm), lambda i, e: (0, 0, i))
    return pl.pallas_call(
        _peer_main_kernel,
        grid=(T // tm, NE // te),
        in_specs=[pl.BlockSpec((D, tm), lambda i, e: (0, i)),
                  pl.BlockSpec((te, D), lambda i, e: (e, 0)),
                  pl.BlockSpec((D, te), lambda i, e: (0, e)),
                  stat, stat, stat, stat],
        out_specs=pl.BlockSpec((D, tm), lambda i, e: (0, i)),
        out_shape=jax.ShapeDtypeStruct((D, T), F32),
        scratch_shapes=[pltpu.VMEM((te, tm), BF16)],
        compiler_params=_cparams("parallel", "arbitrary"),
        name="peer_main",
    )(hT, u, vT, cnt, ca, rb, eb)


def _final_kernel(h_ref, peerT_ref, p_ref, wg_ref, bg_ref, wp_ref, g_ref, b_ref, o_ref, *, alpha):
    r = alpha * h_ref[...] + peerT_ref[...].T
    gate = _sigmoid(_dot(r.astype(BF16), wg_ref[...]) + bg_ref[...])
    ple = gate * _dot(p_ref[...].astype(BF16), wp_ref[...])
    o_ref[...] = _layer_norm(r + ple, g_ref[...], b_ref[...])


def _final(h, peer_t, p2, wg, bg, wp, g, b, alpha):
    T, D = h.shape
    tm = TOKEN_TILE
    row = lambda i: (i, 0)
    full = lambda a: pl.BlockSpec(a.shape, lambda i: (0, 0))
    return pl.pallas_call(
        functools.partial(_final_kernel, alpha=alpha),
        grid=(T // tm,),
        in_specs=[pl.BlockSpec((tm, D), row), pl.BlockSpec((D, tm), lambda i: (0, i)),
                  pl.BlockSpec((tm, p2.shape[1]), row), full(wg), full(bg), full(wp), full(g), full(b)],
        out_specs=pl.BlockSpec((tm, D), row),
        out_shape=jax.ShapeDtypeStruct((T, D), F32),
        compiler_params=_cparams("parallel"),
        name="ple_ln2",
    )(h, peer_t, p2, wg, bg, wp, g, b)


def _rope_tables(seq, rot_dim, theta):
    inv = jnp.power(jnp.float32(theta), -jnp.arange(0, rot_dim, 2, dtype=F32) / rot_dim)
    ang = jnp.arange(seq).astype(F32)[:, None] * inv[None, :]
    ang = jnp.concatenate([ang, ang], axis=-1)
    return jnp.cos(ang), jnp.sin(ang)


def _rotate_half_cols(w):
    r = w.shape[-1]
    return jnp.concatenate([-w[..., r // 2:], w[..., :r // 2]], axis=-1)


def _pad_last(a, before, total):
    pad = [(0, 0)] * (a.ndim - 1) + [(before, total - before - a.shape[-1])]
    return jnp.pad(a, pad)


def kernel(x, p, w_in, b_gates, mla_q_norm, w_q_up, mla_kv_norm, w_kv_up, w_branch_moba, w_branch_mla, w_out, ln1_g, ln1_b, peer_w_q, peer_sub_keys, peer_u, peer_v, w_ple_proj, w_ple_gate, b_ple_gate, ln2_g, ln2_b):
    B, S, D = x.shape
    depth = w_in.shape[0]
    T = B * S
    alpha = (2.0 * depth) ** 0.25
    H, dh = MOBA_HEADS, MOBA_HEAD_DIM

    cos_p, sin_p = _rope_tables(S, MOBA_ROT, MOBA_THETA)
    cos_a = jnp.tile(jnp.concatenate([cos_p, jnp.ones((S, dh - MOBA_ROT), F32)], axis=1), (1, LANES // dh))
    sin_a = jnp.tile(jnp.concatenate([sin_p, jnp.zeros((S, dh - MOBA_ROT), F32)], axis=1), (1, LANES // dh))
    cos_m, sin_m = _rope_tables(S, MLA_ROPE, MLA_THETA)
    cos_b = jnp.concatenate([jnp.ones((S, MLA_NOPE), F32), cos_m,
                             jnp.zeros((S, MLA_PAD - MLA_QK), F32)], axis=1)
    sin_b = _pad_last(sin_m, MLA_NOPE, MLA_PAD)

    o_k, o_v, o_cq = MOBA_W, 2 * MOBA_W, 3 * MOBA_W
    o_ckv = o_cq + MLA_Q_LORA
    o_kr = o_ckv + MLA_KV_LORA
    o_g = o_kr + MLA_ROPE

    def moba_rot(w):
        w3 = w.reshape(D, H, dh)
        return _pad_last(_rotate_half_cols(w3[..., :MOBA_ROT]), 0, dh).reshape(D, H * dh)

    x2 = x.reshape(T, D)
    for i in range(depth):
        w = w_in[i]
        wq, wk, wv = w[:, :o_k], w[:, o_k:o_v], w[:, o_v:o_cq]
        w_moba = jnp.concatenate([wq, moba_rot(wq), wk, moba_rot(wk), wv], axis=1).astype(BF16)
        q_a, k_a, v_a, kmean = _moba_proj(x2, w_moba, cos_a, sin_a, S)

        wkr = w[:, o_kr:o_g]
        wkr2 = jnp.concatenate([_pad_last(wkr, MLA_NOPE, MLA_PAD),
                                _pad_last(_rotate_half_cols(wkr), MLA_NOPE, MLA_PAD)], axis=1).astype(BF16)
        wqu = w_q_up[i].reshape(MLA_Q_LORA, MLA_HEADS, MLA_QK)
        wq_pad = _pad_last(wqu, 0, MLA_PAD).reshape(MLA_Q_LORA, -1).astype(BF16)
        wqr_pad = _pad_last(_rotate_half_cols(wqu[..., MLA_NOPE:]), MLA_NOPE, MLA_PAD)
        wqr_pad = wqr_pad.reshape(MLA_Q_LORA, -1).astype(BF16)
        wkv = w_kv_up[i].reshape(MLA_KV_LORA, MLA_HEADS, MLA_NOPE + MLA_V)
        wkn_pad = _pad_last(wkv[..., :MLA_NOPE], 0, MLA_PAD).reshape(MLA_KV_LORA, -1).astype(BF16)
        wvv = wkv[..., MLA_NOPE:].reshape(MLA_KV_LORA, -1).astype(BF16)
        q_m, k_m, v_m = _mla_proj(
            x2, w[:, o_cq:o_ckv].astype(BF16), w[:, o_ckv:o_kr].astype(BF16), wkr2,
            mla_q_norm[i][None, :], mla_kv_norm[i][None, :], wq_pad, wqr_pad, wkn_pad, wvv, cos_b, sin_b, S)

        y_a = _moba_attn(q_a, k_a, v_a.reshape(B, H, dh, S), kmean.reshape(T // MOBA_BLOCK, MOBA_W), B, S)
        y_b = _mla_attn(q_m, k_m, v_m.reshape(B, MLA_HEADS, MLA_V, S), B, S)

        keys = peer_sub_keys[i].reshape(2 * PEER_HEADS, PEER_N_KEYS, PEER_KEY_DIM // 2).astype(BF16)
        h, hT, sT = _merge(x2, y_a, y_b, w[:, o_g:].astype(BF16), b_gates[i].reshape(1, -1),
                           w_branch_moba[i].astype(BF16), w_branch_mla[i].astype(BF16), w_out[i].astype(BF16),
                           ln1_g[i][None, :], ln1_b[i][None, :], peer_w_q[i].astype(BF16), keys, alpha)
        cnt, ca, rb, eb = _peer_select(sT)
        peer_t = _peer_main(hT, peer_u[i].astype(BF16), peer_v[i].T.astype(BF16), cnt, ca, rb, eb)

        x2 = _final(h, peer_t, p[i].reshape(T, -1), w_ple_gate[i].astype(BF16), b_ple_gate[i][None, :],
                    w_ple_proj[i].astype(BF16), ln2_g[i][None, :], ln2_b[i][None, :], alpha)
    return x2.reshape(B, S, D)
```
